```python
import jax, jax.numpy as jnp
from jax import lax
import numpy as np

D_MODEL = 4096
BATCH = 4
SEQ = 2048
DEPTH = 1
DEC_BATCH = 32
DEC_SEQ = 4
PAST_LEN = 8192
PAGE_SIZE = 128

M_HEADS = 4
M_DV = D_MODEL // (2 * M_HEADS)
M_DQK = M_DV // 2
M_WIDTH = M_HEADS * M_DV
M_CHUNK = 64
GATE_CAP = 15.0
N_HEADS = 16
N_KV = 4
N_HPG = N_HEADS // N_KV
N_DV = D_MODEL // (2 * N_HEADS)
N_DQK = 192
N_KVW = N_DQK + N_DV
N_WIDTH = N_HEADS * N_DV
CMP_LEN = 32
CMP_STRIDE = 16
SEL_LEN = 64
SEL_TOPK = 16
N_LOCAL = 2
WINDOW = 512
NSA_QBLK = 64
MIX_WIDTH = M_WIDTH + N_WIDTH
N_EXPERTS = 32
TOP_K = 4
D_EXPERT = D_MODEL
SWIGLU_LIMIT = 7.0
SWIGLU_ALPHA = 1.702
MOE_BLK = 128
PLE_DIM = 256
EPS = 1e-6

kernel_name = 'hymba_mlstm_nsa_moe_decode_step'


def in_sizes():
    return (M_HEADS * M_DQK, M_HEADS * M_DQK, M_WIDTH, M_WIDTH, M_HEADS, M_HEADS,
            N_HEADS * N_DQK, N_KV * N_KVW, N_KV * N_KVW, N_KV * N_KVW, 3 * N_HEADS)


def rmsnorm(x, g):
    xf = x.astype(jnp.float32)
    y = xf * lax.rsqrt(jnp.mean(xf * xf, axis=-1, keepdims=True) + EPS)
    return (y * g.astype(jnp.float32)).astype(x.dtype)


def softcap(x, cap):
    return cap * jnp.tanh(x / cap)


def masked_softmax(s, mask):
    s = jnp.where(mask, s.astype(jnp.float32), -jnp.inf)
    mx = jnp.max(s, axis=-1, keepdims=True)
    e = jnp.exp(s - jnp.where(jnp.isfinite(mx), mx, 0.0))
    den = jnp.sum(e, axis=-1, keepdims=True)
    return e / jnp.where(den > 0, den, 1.0)


def mlstm_chunk(carry, inp):
    c, n, m = carry
    q, k, v, ig, lf = inp
    L = q.shape[2]
    b = jnp.cumsum(lf, axis=-1)
    causal = jnp.tril(jnp.ones((L, L), bool))
    dmat = jnp.where(causal, b[..., :, None] - b[..., None, :] + ig[..., None, :], -jnp.inf)
    inter = b + m[..., None]
    m_t = jnp.maximum(inter, jnp.max(dmat, axis=-1))
    w_intra = jnp.exp(dmat - m_t[..., None])
    w_inter = jnp.exp(inter - m_t)
    s = jnp.einsum('bhtd,bhsd->bhts', q, k) * w_intra
    num = jnp.einsum('bhts,bhsv->bhtv', s, v) + w_inter[..., None] * jnp.einsum('bhtd,bhdv->bhtv', q, c)
    den = jnp.sum(s, axis=-1) + w_inter * jnp.einsum('bhtd,bhd->bht', q, n)
    h = num / jnp.maximum(jnp.abs(den), jnp.exp(-m_t))[..., None]
    b_last = b[..., -1]
    g = b_last[..., None] - b + ig
    m_new = jnp.maximum(b_last + m, jnp.max(g, axis=-1))
    wk = jnp.exp(g - m_new[..., None])
    decay = jnp.exp(b_last + m - m_new)
    c_new = decay[..., None, None] * c + jnp.einsum('bhs,bhsd,bhsv->bhdv', wk, k, v)
    n_new = decay[..., None] * n + jnp.einsum('bhs,bhsd->bhd', wk, k)
    return (c_new, n_new, m_new), h


def to_chunks(a, nc, L):
    a = a.astype(jnp.float32).reshape((a.shape[0], nc, L) + a.shape[2:])
    return jnp.moveaxis(jnp.moveaxis(a, 3, 2), 1, 0)


def mlstm_mix(q, k, v, ig, lf, c0, n0, m0):
    B, T, H, _ = q.shape
    L = M_CHUNK if T % M_CHUNK == 0 else T
    nc = T // L
    xs = (to_chunks(q, nc, L), to_chunks(k, nc, L), to_chunks(v, nc, L),
          to_chunks(ig, nc, L), to_chunks(lf, nc, L))
    init = (c0.astype(jnp.float32), n0.astype(jnp.float32), m0.astype(jnp.float32))
    (c1, n1, m1), hs = lax.scan(mlstm_chunk, init, xs)
    h = jnp.moveaxis(jnp.moveaxis(hs, 0, 1), 2, 3).reshape(B, T, H, -1)
    return h, c1, n1, m1


def compress_blocks(x_raw, pe, w1, w2):
    B, Tk, G, d = x_raw.shape
    r = CMP_LEN // CMP_STRIDE
    n_sub = Tk // CMP_STRIDE
    n_cmp = n_sub - r + 1
    sub = x_raw[:, :n_sub * CMP_STRIDE].reshape(B, n_sub, CMP_STRIDE, G, d)
    sub = jnp.moveaxis(sub, 3, 2).reshape(B, n_sub, G, CMP_STRIDE * d)
    parts = jnp.einsum('bngx,rxh->rbngh', sub, w1.reshape(r, CMP_STRIDE * d, -1))
    pre = parts[0, :, :n_cmp]
    for j in range(1, r):
        pre = pre + parts[j, :, j:j + n_cmp]
    pre = pre + pe.reshape(-1) @ w1
    return jax.nn.gelu(pre) @ w2


def nsa_attend(q, q_pos0, kc, vc, ks, vs, kw, vw, kw_pos0, gates):
    B, Tq = q.shape[:2]
    QB = NSA_QBLK if Tq % NSA_QBLK == 0 else Tq
    nqb = Tq // QB
    scale = N_DQK ** -0.5
    n_cmp = kc.shape[1]
    cmp_start = jnp.arange(n_cmp) * CMP_STRIDE
    cmp_end = cmp_start + CMP_LEN - 1
    Tk = ks.shape[1]
    n_sel = -(-Tk // SEL_LEN)
    pad = n_sel * SEL_LEN - Tk
    ksb = jnp.moveaxis(jnp.pad(ks, ((0, 0), (0, pad), (0, 0), (0, 0))).reshape(B, n_sel, SEL_LEN, N_KV, N_DQK), 3, 1)
    vsb = jnp.moveaxis(jnp.pad(vs, ((0, 0), (0, pad), (0, 0), (0, 0))).reshape(B, n_sel, SEL_LEN, N_KV, N_DV), 3, 1)
    top = min(SEL_TOPK, n_sel)
    sel_start = jnp.arange(n_sel) * SEL_LEN
    overlap = ((cmp_start[:, None] <= sel_start[None, :] + SEL_LEN - 1) &
               (cmp_end[:, None] >= sel_start[None, :])).astype(jnp.float32)
    kwp = jnp.pad(kw, ((0, 0), (WINDOW, 0), (0, 0), (0, 0)))
    vwp = jnp.pad(vw, ((0, 0), (WINDOW, 0), (0, 0), (0, 0)))
    WK = WINDOW + QB
    qg = q.reshape(B, Tq, N_KV, N_HPG, N_DQK)
    gg = gates.reshape(B, Tq, N_KV, N_HPG, 3)
    gather_blocks = jax.vmap(jax.vmap(lambda blk, ix: blk[ix]))
    jj = jnp.arange(n_sel)

    def block(j):
        q0 = j * QB
        qb = lax.dynamic_slice_in_dim(qg, q0, QB, axis=1)
        gb = lax.dynamic_slice_in_dim(gg, q0, QB, axis=1)
        qpos = q_pos0 + q0 + jnp.arange(QB)
        s_c = jnp.einsum('bqghd,bngd->bghqn', qb, kc).astype(jnp.float32) * scale
        p_c = masked_softmax(s_c, cmp_end[None, :] <= qpos[:, None])
        o_c = jnp.einsum('bghqn,bngd->bqghd', p_c.astype(vc.dtype), vc)
        imp = jnp.einsum('bghqn,nj->bgqj', p_c, overlap)
        cur = qpos // SEL_LEN
        valid = sel_start[None, :] <= qpos[:, None]
        forced = (jj[None, :] == 0) | ((jj[None, :] <= cur[:, None]) & (jj[None, :] > cur[:, None] - N_LOCAL))
        rank = jnp.where(forced, jnp.inf, jnp.where(valid, imp, -jnp.inf))
        _, idx = lax.top_k(rank, top)
        kg = gather_blocks(ksb, idx)
        vg = gather_blocks(vsb, idx).reshape(B, N_KV, QB, top * SEL_LEN, N_DV)
        kpos = (idx[..., None] * SEL_LEN + jnp.arange(SEL_LEN)).reshape(B, N_KV, QB, top * SEL_LEN)
        s_s = jnp.einsum('bqghd,bgqtkd->bghqtk', qb, kg).astype(jnp.float32) * scale
        s_s = s_s.reshape(B, N_KV, N_HPG, QB, top * SEL_LEN)
        p_s = masked_softmax(s_s, (kpos <= qpos[None, None, :, None])[:, :, None])
        o_s = jnp.einsum('bghqx,bgqxd->bqghd', p_s.astype(vg.dtype), vg)
        start = q_pos0 + q0 - kw_pos0
        kbw = lax.dynamic_slice_in_dim(kwp, start, WK, axis=1)
        vbw = lax.dynamic_slice_in_dim(vwp, start, WK, axis=1)
        wpos = q_pos0 + q0 - WINDOW + jnp.arange(WK)
        wmask = ((wpos[None, :] <= qpos[:, None]) & (wpos[None, :] > qpos[:, None] - WINDOW) &
                 (wpos[None, :] >= kw_pos0))
        s_w = jnp.einsum('bqghd,bkgd->bghqk', qb, kbw).astype(jnp.float32) * scale
        p_w = masked_softmax(s_w, wmask)
        o_w = jnp.einsum('bghqk,bkgd->bqghd', p_w.astype(vbw.dtype), vbw)
        out = gb[..., 0:1] * o_c + gb[..., 1:2] * o_s + gb[..., 2:3] * o_w
        return out.astype(q.dtype)

    outs = lax.map(block, jnp.arange(nqb))
    return jnp.moveaxis(outs, 0, 1).reshape(B, Tq, N_WIDTH)


def moe_ffn(x, prm):
    n, d = x.shape
    logits = (x @ prm['w_router']).astype(jnp.float32) + prm['b_router'].astype(jnp.float32)
    top_logit, top_e = lax.top_k(logits, TOP_K)
    gate = jax.nn.softmax(top_logit, axis=-1)
    nk = n * TOP_K
    flat_e = top_e.reshape(nk)
    order = jnp.argsort(flat_e)
    sorted_e = flat_e[order]
    counts = jnp.bincount(flat_e, length=N_EXPERTS)
    padded = (counts + MOE_BLK - 1) // MOE_BLK * MOE_BLK
    start = jnp.cumsum(counts) - counts
    pad_end = jnp.cumsum(padded)
    pad_start = pad_end - padded
    dest = (pad_start[sorted_e] + jnp.arange(nk) - start[sorted_e]).astype(jnp.int32)
    n_blocks = -(-(nk + N_EXPERTS * (MOE_BLK - 1)) // MOE_BLK)
    slot_tok = jnp.full((n_blocks * MOE_BLK,), n, jnp.int32).at[dest].set((order // TOP_K).astype(jnp.int32))
    block_e = jnp.minimum(jnp.searchsorted(pad_end, jnp.arange(n_blocks) * MOE_BLK, side='right'), N_EXPERTS - 1)
    x_ext = jnp.concatenate([x, jnp.zeros((1, d), x.dtype)], axis=0)

    def expert_block(args):
        tok, e = args
        xb = x_ext[tok]
        g = jnp.minimum(xb @ prm['w_gate'][e] + prm['b_gate'][e], SWIGLU_LIMIT)
        u = jnp.clip(xb @ prm['w_up'][e] + prm['b_up'][e], -SWIGLU_LIMIT, SWIGLU_LIMIT)
        hdn = g * jax.nn.sigmoid(SWIGLU_ALPHA * g) * (u + 1)
        return hdn @ prm['w_down'][e] + prm['b_down'][e]

    y_slots = lax.map(expert_block, (slot_tok.reshape(n_blocks, MOE_BLK), block_e)).reshape(n_blocks * MOE_BLK, d)
    slot_of = jnp.zeros((nk,), jnp.int32).at[order].set(dest)
    return jnp.einsum('nkd,nk->nd', y_slots[slot_of].reshape(n, TOP_K, d), gate.astype(x.dtype))


def decoder_layer(x, ple, past_cmp, past_slc, win_buf, c0, n0, m0, pos0, prm):
    B, T, _ = x.shape
    a = rmsnorm(x, prm['norm_mix'])
    z = a @ prm['w_in']
    mq, mk, mv, mo, mi, mf, nq, kvc, kvs, kvw, ng = jnp.split(z, np.cumsum(in_sizes())[:-1].tolist(), axis=-1)
    mq = mq.reshape(B, T, M_HEADS, M_DQK)
    mk = mk.reshape(B, T, M_HEADS, M_DQK) * (M_DQK ** -0.5)
    mv = mv.reshape(B, T, M_HEADS, M_DV)
    ig = softcap(mi.astype(jnp.float32) + prm['b_mlstm_i'].astype(jnp.float32), GATE_CAP)
    lf = jax.nn.log_sigmoid(softcap(mf.astype(jnp.float32) + prm['b_mlstm_f'].astype(jnp.float32), GATE_CAP))
    h, c1, n1, m1 = mlstm_mix(mq, mk, mv, ig, lf, c0, n0, m0)
    h = rmsnorm(h, prm['mlstm_out_norm']).reshape(B, T, M_WIDTH)
    y_m = (h * jax.nn.sigmoid(mo.astype(jnp.float32))).astype(x.dtype)
    q = rmsnorm(nq.reshape(B, T, N_HEADS, N_DQK), prm['q_norm'])
    kvc = kvc.reshape(B, T, N_KV, N_KVW)
    kvs = kvs.reshape(B, T, N_KV, N_KVW)
    kvs = jnp.concatenate([rmsnorm(kvs[..., :N_DQK], prm['k_norm_slc']), kvs[..., N_DQK:]], axis=-1)
    kvw = kvw.reshape(B, T, N_KV, N_KVW)
    kvw = jnp.concatenate([rmsnorm(kvw[..., :N_DQK], prm['k_norm_win']), kvw[..., N_DQK:]], axis=-1)
    all_c = kvc if past_cmp is None else jnp.concatenate([past_cmp.astype(x.dtype), kvc], axis=1)
    all_s = kvs if past_slc is None else jnp.concatenate([past_slc.astype(x.dtype), kvs], axis=1)
    all_w = kvw if win_buf is None else jnp.concatenate([win_buf.astype(x.dtype), kvw], axis=1)
    win_pos0 = pos0 + T - all_w.shape[1]
    kc = rmsnorm(compress_blocks(all_c[..., :N_DQK], prm['cmp_pe_k'], prm['cmp_w1_k'], prm['cmp_w2_k']), prm['k_norm_cmp'])
    vc = compress_blocks(all_c[..., N_DQK:], prm['cmp_pe_v'], prm['cmp_w1_v'], prm['cmp_w2_v'])
    gates = jax.nn.sigmoid(ng.astype(jnp.float32)).reshape(B, T, N_HEADS, 3)
    y_n = nsa_attend(q, pos0, kc, vc, all_s[..., :N_DQK], all_s[..., N_DQK:],
                     all_w[..., :N_DQK], all_w[..., N_DQK:], win_pos0, gates)
    x = x + jnp.concatenate([y_m, y_n], axis=-1) @ prm['w_out']
    f = rmsnorm(x, prm['norm_ffn'])
    x = x + moe_ffn(f.reshape(B * T, D_MODEL), prm).reshape(B, T, D_MODEL)
    u = rmsnorm(x, prm['norm_ple'])
    x = x + (jax.nn.sigmoid((u @ prm['w_ple_gate']).astype(jnp.float32)) *
             (ple @ prm['w_ple_proj']).astype(jnp.float32)).astype(x.dtype)
    keep = min(WINDOW, pos0 + T)
    return x, kvc, kvs, all_w[:, all_w.shape[1] - keep:], c1, n1, m1


def setup_inputs(seed: int = 0) -> dict:
    key = jax.random.key(seed)
    ks = jax.random.split(key, 40)
    f32 = jnp.float32
    n_pages = PAST_LEN // PAGE_SIZE
    n_pool = (DEC_BATCH * n_pages * 5 + 3) // 4
    wbuf = min(WINDOW, PAST_LEN)
    n_in = sum(in_sizes())

    def nrm(k, shape, scale=1.0):
        return jax.random.normal(k, shape, f32) * scale

    def gain(k, shape):
        return 1.0 + 0.05 * jax.random.normal(k, shape, f32)

    page_table = jax.random.permutation(ks[10], n_pool)[:DEC_BATCH * n_pages].reshape(DEC_BATCH, n_pages).astype(jnp.int32)
    return {
        'x_prompt': nrm(ks[0], (BATCH, SEQ, D_MODEL)),
        'x_sample': nrm(ks[1], (DEC_BATCH, DEC_SEQ, D_MODEL)),
        'p_prompt': nrm(ks[2], (DEPTH, BATCH, SEQ, PLE_DIM)),
        'p_sample': nrm(ks[3], (DEPTH, DEC_BATCH, DEC_SEQ, PLE_DIM)),
        'cache_kv_cmp': nrm(ks[4], (DEPTH, n_pool, PAGE_SIZE, N_KV, N_KVW)),
        'cache_kv_slc': nrm(ks[5], (DEPTH, n_pool, PAGE_SIZE, N_KV, N_KVW)),
        'cache_kv_win': nrm(ks[6], (DEPTH, DEC_BATCH, wbuf, N_KV, N_KVW)),
        'state_mlstm_C': nrm(ks[7], (DEPTH, DEC_BATCH, M_HEADS, M_DQK, M_DV)),
        'state_mlstm_n': nrm(ks[8], (DEPTH, DEC_BATCH, M_HEADS, M_DQK)),
        'state_mlstm_m': nrm(ks[9], (DEPTH, DEC_BATCH, M_HEADS)),
        'page_table': page_table,
        'norm_mix': gain(ks[11], (DEPTH, D_MODEL)),
        'w_in': nrm(ks[12], (DEPTH, D_MODEL, n_in), D_MODEL ** -0.5),
        'b_mlstm_i': nrm(ks[13], (DEPTH, M_HEADS), 0.1),
        'b_mlstm_f': 3.0 + 3.0 * jax.random.uniform(ks[14], (DEPTH, M_HEADS), f32),
        'mlstm_out_norm': gain(ks[15], (DEPTH, M_HEADS, M_DV)),
        'q_norm': gain(ks[16], (DEPTH, N_DQK)),
        'k_norm_cmp': gain(ks[17], (DEPTH, N_DQK)),
        'k_norm_slc': gain(ks[18], (DEPTH, N_DQK)),
        'k_norm_win': gain(ks[19], (DEPTH, N_DQK)),
        'cmp_pe_k': nrm(ks[20], (DEPTH, CMP_LEN, N_DQK), 0.1),
        'cmp_w1_k': nrm(ks[21], (DEPTH, CMP_LEN * N_DQK, N_DQK), (CMP_LEN * N_DQK) ** -0.5),
        'cmp_w2_k': nrm(ks[22], (DEPTH, N_DQK, N_DQK), N_DQK ** -0.5),
        'cmp_pe_v': nrm(ks[23], (DEPTH, CMP_LEN, N_DV), 0.1),
        'cmp_w1_v': nrm(ks[24], (DEPTH, CMP_LEN * N_DV, N_DV), (CMP_LEN * N_DV) ** -0.5),
        'cmp_w2_v': nrm(ks[25], (DEPTH, N_DV, N_DV), N_DV ** -0.5),
        'w_out': nrm(ks[26], (DEPTH, MIX_WIDTH, D_MODEL), MIX_WIDTH ** -0.5),
        'norm_ffn': gain(ks[27], (DEPTH, D_MODEL)),
        'w_router': nrm(ks[28], (DEPTH, D_MODEL, N_EXPERTS), D_MODEL ** -0.5),
        'b_router': nrm(ks[29], (DEPTH, N_EXPERTS), 0.01),
        'w_gate': nrm(ks[30], (DEPTH, N_EXPERTS, D_MODEL, D_EXPERT), D_MODEL ** -0.5),
        'b_gate': nrm(ks[31], (DEPTH, N_EXPERTS, D_EXPERT), 0.02),
        'w_up': nrm(ks[32], (DEPTH, N_EXPERTS, D_MODEL, D_EXPERT), D_MODEL ** -0.5),
        'b_up': nrm(ks[33], (DEPTH, N_EXPERTS, D_EXPERT), 0.02),
        'w_down': nrm(ks[34], (DEPTH, N_EXPERTS, D_EXPERT, D_MODEL), D_EXPERT ** -0.5),
        'b_down': nrm(ks[35], (DEPTH, N_EXPERTS, D_MODEL), 0.02),
        'norm_ple': gain(ks[36], (DEPTH, D_MODEL)),
        'w_ple_gate': nrm(ks[37], (DEPTH, D_MODEL, D_MODEL), D_MODEL ** -0.5),
        'w_ple_proj': nrm(ks[38], (DEPTH, PLE_DIM, D_MODEL), PLE_DIM ** -0.5),
    }


def reference(x_prompt, x_sample, p_prompt, p_sample, cache_kv_cmp, cache_kv_slc, cache_kv_win,
              state_mlstm_C, state_mlstm_n, state_mlstm_m, page_table,
              norm_mix, w_in, b_mlstm_i, b_mlstm_f, mlstm_out_norm, q_norm, k_norm_cmp, k_norm_slc,
              k_norm_win, cmp_pe_k, cmp_w1_k, cmp_w2_k, cmp_pe_v, cmp_w1_v, cmp_w2_v, w_out, norm_ffn,
              w_router, b_router, w_gate, b_gate, w_up, b_up, w_down, b_down, norm_ple, w_ple_gate,
              w_ple_proj):
    y_p = x_prompt
    y_s = x_sample
    bp = x_prompt.shape[0]
    db = x_sample.shape[0]
    n_past = page_table.shape[1] * cache_kv_cmp.shape[2]
    past_len = n_past
    p_cmp, p_slc, p_win, p_c, p_n, p_m = [], [], [], [], [], []
    s_cmp, s_slc, s_win, s_c, s_n, s_m = [], [], [], [], [], []
    for l in range(DEPTH):
        prm = dict(norm_mix=norm_mix[l], w_in=w_in[l], b_mlstm_i=b_mlstm_i[l], b_mlstm_f=b_mlstm_f[l],
                   mlstm_out_norm=mlstm_out_norm[l], q_norm=q_norm[l], k_norm_cmp=k_norm_cmp[l],
                   k_norm_slc=k_norm_slc[l], k_norm_win=k_norm_win[l], cmp_pe_k=cmp_pe_k[l],
                   cmp_w1_k=cmp_w1_k[l], cmp_w2_k=cmp_w2_k[l], cmp_pe_v=cmp_pe_v[l], cmp_w1_v=cmp_w1_v[l],
                   cmp_w2_v=cmp_w2_v[l], w_out=w_out[l], norm_ffn=norm_ffn[l], w_router=w_router[l],
                   b_router=b_router[l], w_gate=w_gate[l], b_gate=b_gate[l], w_up=w_up[l], b_up=b_up[l],
                   w_down=w_down[l], b_down=b_down[l], norm_ple=norm_ple[l], w_ple_gate=w_ple_gate[l],
                   w_ple_proj=w_ple_proj[l])
        c0 = jnp.zeros((bp, M_HEADS, M_DQK, M_DV), jnp.float32)
        n0 = jnp.zeros((bp, M_HEADS, M_DQK), jnp.float32)
        m0 = jnp.zeros((bp, M_HEADS), jnp.float32)
        y_p, a1, a2, a3, a4, a5, a6 = decoder_layer(y_p, p_prompt[l], None, None, None, c0, n0, m0, 0, prm)
        p_cmp.append(a1); p_slc.append(a2); p_win.append(a3); p_c.append(a4); p_n.append(a5); p_m.append(a6)
        past_c = cache_kv_cmp[l][page_table].reshape(db, n_past, N_KV, N_KVW)
        past_s = cache_kv_slc[l][page_table].reshape(db, n_past, N_KV, N_KVW)
        y_s, b1, b2, b3, b4, b5, b6 = decoder_layer(y_s, p_sample[l], past_c, past_s, cache_kv_win[l],
                                                    state_mlstm_C[l], state_mlstm_n[l], state_mlstm_m[l],
                                                    past_len, prm)
        s_cmp.append(b1); s_slc.append(b2); s_win.append(b3); s_c.append(b4); s_n.append(b5); s_m.append(b6)
    return (y_p, y_s,
            jnp.stack(p_cmp), jnp.stack(p_slc), jnp.stack(p_win), jnp.stack(p_c), jnp.stack(p_n), jnp.stack(p_m),
            jnp.stack(s_cmp), jnp.stack(s_slc), jnp.stack(s_win), jnp.stack(s_c), jnp.stack(s_n), jnp.stack(s_m))
```

```python
import functools

import numpy as np
import jax
import jax.numpy as jnp
from jax import lax
from jax.experimental import pallas as pl
from jax.experimental.pallas import tpu as pltpu

D_MODEL = 4096
M_HEADS = 4
M_DV = D_MODEL // (2 * M_HEADS)
M_DQK = M_DV // 2
M_WIDTH = M_HEADS * M_DV
M_CHUNK = 64
GATE_CAP = 15.0
N_HEADS = 16
N_KV = 4
N_HPG = N_HEADS // N_KV
N_DV = D_MODEL // (2 * N_HEADS)
N_DQK = 192
N_KVW = N_DQK + N_DV
N_WIDTH = N_HEADS * N_DV
CMP_LEN = 32
CMP_STRIDE = 16
SEL_LEN = 64
SEL_TOPK = 16
N_LOCAL = 2
WINDOW = 512
NSA_QBLK = 64
N_EXPERTS = 32
TOP_K = 4
SWIGLU_LIMIT = 7.0
SWIGLU_ALPHA = 1.702
EPS = 1e-6

IN_SIZES = (M_HEADS * M_DQK, M_HEADS * M_DQK, M_WIDTH, M_WIDTH, M_HEADS, M_HEADS,
            N_HEADS * N_DQK, N_KV * N_KVW, N_KV * N_KVW, N_KV * N_KVW, 3 * N_HEADS)
N_IN = sum(IN_SIZES)

VMEM_LIMIT_BYTES = 56 * 1024 * 1024
LANE = 128
MM_TN = 512
MOE_TM = 256
MOE_TN = 512

BF16 = jnp.bfloat16
F32 = jnp.float32


def _round_up(a, b):
    return (a + b - 1) // b * b


def _mm_kernel(a_ref, g_ref, w_ref, o_ref, a_bf, *, norm):
    @pl.when(pl.program_id(1) == 0)
    def _():
        a = a_ref[...]
        if norm:
            ms = jnp.mean(a * a, axis=-1, keepdims=True)
            a = a * lax.rsqrt(ms + EPS) * g_ref[...]
        a_bf[...] = a.astype(BF16)

    o_ref[...] = jnp.dot(a_bf[...], w_ref[...], preferred_element_type=F32)


def _matmul(a, gain, w_bf, *, tm, tn=MM_TN, norm):
    m, k = a.shape
    n = w_bf.shape[1]
    assert m % tm == 0 and n % tn == 0 and w_bf.shape[0] == k
    g = (jnp.ones((k,), F32) if gain is None else gain.astype(F32)).reshape(1, k)
    return pl.pallas_call(
        functools.partial(_mm_kernel, norm=norm),
        grid=(m // tm, n // tn),
        in_specs=[pl.BlockSpec((tm, k), lambda i, j: (i, 0)),
                  pl.BlockSpec((1, k), lambda i, j: (0, 0)),
                  pl.BlockSpec((k, tn), lambda i, j: (0, j))],
        out_specs=pl.BlockSpec((tm, tn), lambda i, j: (i, j)),
        out_shape=jax.ShapeDtypeStruct((m, n), F32),
        scratch_shapes=[pltpu.VMEM((tm, k), BF16)],
        compiler_params=pltpu.CompilerParams(
            dimension_semantics=("parallel", "arbitrary"),
            vmem_limit_bytes=VMEM_LIMIT_BYTES),
        name="dense_proj",
    )(a, g, w_bf)


def _router_kernel(x_ref, g_ref, whi_ref, wlo_ref, b_ref, f_ref, l_ref):
    x = x_ref[...]
    ms = jnp.mean(x * x, axis=-1, keepdims=True)
    f = x * lax.rsqrt(ms + EPS) * g_ref[...]
    f_hi = f.astype(BF16)
    f_lo = (f - f_hi.astype(F32)).astype(BF16)
    f_ref[...] = f_hi
    acc = jnp.dot(f_hi, whi_ref[...], preferred_element_type=F32)
    acc = acc + jnp.dot(f_lo, whi_ref[...], preferred_element_type=F32)
    acc = acc + jnp.dot(f_hi, wlo_ref[...], preferred_element_type=F32)
    l_ref[...] = acc + b_ref[...]


def _router(x, gain, w_router, b_router, *, tm):
    m, k = x.shape
    e = w_router.shape[1]
    w_hi = w_router.astype(BF16)
    w_lo = (w_router - w_hi.astype(F32)).astype(BF16)
    return pl.pallas_call(
        _router_kernel,
        grid=(m // tm,),
        in_specs=[pl.BlockSpec((tm, k), lambda i: (i, 0)),
                  pl.BlockSpec((1, k), lambda i: (0, 0)),
                  pl.BlockSpec((k, e), lambda i: (0, 0)),
                  pl.BlockSpec((k, e), lambda i: (0, 0)),
                  pl.BlockSpec((1, e), lambda i: (0, 0))],
        out_specs=[pl.BlockSpec((tm, k), lambda i: (i, 0)),
                   pl.BlockSpec((tm, e), lambda i: (i, 0))],
        out_shape=[jax.ShapeDtypeStruct((m, k), BF16),
                   jax.ShapeDtypeStruct((m, e), F32)],
        compiler_params=pltpu.CompilerParams(
            dimension_semantics=("parallel",),
            vmem_limit_bytes=VMEM_LIMIT_BYTES),
        name="moe_router",
    )(x, gain.reshape(1, k), w_hi, w_lo, b_router.reshape(1, e))


def _item_flags(blk_ref, exp_ref, st_ref, en_ref, tm):
    w = pl.program_id(1)
    e = exp_ref[w]
    b = blk_ref[w]
    prev = jnp.maximum(w - 1, 0)
    new_expert = jnp.logical_or(w == 0, exp_ref[prev] != e)
    new_block = jnp.logical_or(w == 0, blk_ref[prev] != b)
    rows = b * tm + lax.broadcasted_iota(jnp.int32, (tm, 1), 0)
    mask = jnp.logical_and(rows >= st_ref[e], rows < en_ref[e])
    return w, new_expert, new_block, mask


def _moe_up_kernel(blk_ref, exp_ref, st_ref, en_ref, tot_ref,
                   x_ref, wg_ref, wu_ref, bg_ref, bu_ref, h_ref, wg_bf, wu_bf, *, tm):
    w, new_expert, new_block, mask = _item_flags(blk_ref, exp_ref, st_ref, en_ref, tm)

    @pl.when(w < tot_ref[0])
    def _():
        @pl.when(new_expert)
        def _():
            wg_bf[...] = wg_ref[0].astype(BF16)
            wu_bf[...] = wu_ref[0].astype(BF16)

        x = x_ref[...]
        g = jnp.dot(x, wg_bf[...], preferred_element_type=F32) + bg_ref[0]
        u = jnp.dot(x, wu_bf[...], preferred_element_type=F32) + bu_ref[0]
        g = jnp.minimum(g, SWIGLU_LIMIT)
        u = jnp.clip(u, -SWIGLU_LIMIT, SWIGLU_LIMIT)
        hdn = (g * jax.nn.sigmoid(SWIGLU_ALPHA * g) * (u + 1.0)).astype(h_ref.dtype)
        keep = jnp.where(new_block, jnp.zeros_like(hdn), h_ref[...])
        h_ref[...] = jnp.where(mask, hdn, keep)


def _moe_down_kernel(blk_ref, exp_ref, st_ref, en_ref, tot_ref,
                     h_ref, wd_ref, bd_ref, y_ref, wd_bf, *, tm):
    w, new_expert, new_block, mask = _item_flags(blk_ref, exp_ref, st_ref, en_ref, tm)

    @pl.when(w < tot_ref[0])
    def _():
        @pl.when(new_expert)
        def _():
            wd_bf[...] = wd_ref[0].astype(BF16)

        y = jnp.dot(h_ref[...], wd_bf[...], preferred_element_type=F32) + bd_ref[0]
        keep = jnp.where(new_block, jnp.zeros_like(y), y_ref[...])
        y_ref[...] = jnp.where(mask, y, keep)


def _moe_experts(x_sorted, sched, w_gate, b_gate, w_up, b_up, w_down, b_down):
    blk, exp, starts, ends, total = sched
    rows, d = x_sorted.shape
    n_e, _, d_h = w_gate.shape
    tm, tn = MOE_TM, MOE_TN
    n_items = blk.shape[0]
    params = pltpu.CompilerParams(dimension_semantics=("arbitrary", "arbitrary"),
                                  vmem_limit_bytes=VMEM_LIMIT_BYTES)

    row_spec = pl.BlockSpec((tm, d), lambda j, w, blk, exp, st, en, tot: (blk[w], 0))
    w_spec = pl.BlockSpec((1, d, tn), lambda j, w, blk, exp, st, en, tot: (exp[w], 0, j))
    b_spec = pl.BlockSpec((1, 1, tn), lambda j, w, blk, exp, st, en, tot: (exp[w], 0, j))
    out_spec = pl.BlockSpec((tm, tn), lambda j, w, blk, exp, st, en, tot: (blk[w], j))

    hidden = pl.pallas_call(
        functools.partial(_moe_up_kernel, tm=tm),
        grid_spec=pltpu.PrefetchScalarGridSpec(
            num_scalar_prefetch=5,
            grid=(d_h // tn, n_items),
            in_specs=[row_spec, w_spec, w_spec, b_spec, b_spec],
            out_specs=out_spec,
            scratch_shapes=[pltpu.VMEM((d, tn), BF16), pltpu.VMEM((d, tn), BF16)]),
        out_shape=jax.ShapeDtypeStruct((rows, d_h), BF16),
        compiler_params=params,
        name="moe_up",
    )(blk, exp, starts, ends, total, x_sorted, w_gate, w_up,
      b_gate.reshape(n_e, 1, d_h), b_up.reshape(n_e, 1, d_h))

    return pl.pallas_call(
        functools.partial(_moe_down_kernel, tm=tm),
        grid_spec=pltpu.PrefetchScalarGridSpec(
            num_scalar_prefetch=5,
            grid=(d // tn, n_items),
            in_specs=[row_spec, w_spec, b_spec],
            out_specs=out_spec,
            scratch_shapes=[pltpu.VMEM((d_h, tn), BF16)]),
        out_shape=jax.ShapeDtypeStruct((rows, d), F32),
        compiler_params=params,
        name="moe_down",
    )(blk, exp, starts, ends, total, hidden, w_down, b_down.reshape(n_e, 1, d))


def _moe_schedule(top_e, tm):
    nk = top_e.size
    assert nk % tm == 0
    n_blocks = nk // tm
    n_items = n_blocks + N_EXPERTS - 1
    flat_e = top_e.reshape(nk)
    order = jnp.argsort(flat_e)
    counts = jnp.bincount(flat_e, length=N_EXPERTS).astype(jnp.int32)
    ends = jnp.cumsum(counts).astype(jnp.int32)
    starts = ends - counts
    first_blk = starts // tm
    last_blk = jnp.where(counts > 0, (ends - 1) // tm, first_blk - 1)
    n_it = last_blk - first_blk + 1
    it_end = jnp.cumsum(n_it).astype(jnp.int32)
    it_start = it_end - n_it
    total = it_end[-1]
    w = jnp.arange(n_items, dtype=jnp.int32)
    w_c = jnp.minimum(w, total - 1)
    e_w = jnp.minimum(jnp.searchsorted(it_end, w_c, side='right'), N_EXPERTS - 1).astype(jnp.int32)
    blk_w = (first_blk[e_w] + (w_c - it_start[e_w])).astype(jnp.int32)
    slot_of = jnp.zeros((nk,), jnp.int32).at[order].set(jnp.arange(nk, dtype=jnp.int32))
    return order, slot_of, (blk_w, e_w, starts, ends, total.reshape(1))


def _moe_ffn(f_bf, logits, prm):
    n, d = f_bf.shape
    top_logit, top_e = lax.top_k(logits, TOP_K)
    gate = jax.nn.softmax(top_logit, axis=-1)
    order, slot_of, sched = _moe_schedule(top_e, MOE_TM)
    x_sorted = f_bf[order // TOP_K]
    y_sorted = _moe_experts(x_sorted, sched, prm['w_gate'], prm['b_gate'], prm['w_up'], prm['b_up'],
                            prm['w_down'], prm['b_down'])
    return jnp.einsum('nkd,nk->nd', y_sorted[slot_of].reshape(n, TOP_K, d), gate)


def _rmsnorm(x, g):
    xf = x.astype(F32)
    y = xf * lax.rsqrt(jnp.mean(xf * xf, axis=-1, keepdims=True) + EPS)
    return (y * g.astype(F32)).astype(x.dtype)


def _softcap(x, cap):
    return cap * jnp.tanh(x / cap)


def _masked_softmax(s, mask):
    s = jnp.where(mask, s.astype(F32), -jnp.inf)
    mx = jnp.max(s, axis=-1, keepdims=True)
    e = jnp.exp(s - jnp.where(jnp.isfinite(mx), mx, 0.0))
    den = jnp.sum(e, axis=-1, keepdims=True)
    return e / jnp.where(den > 0, den, 1.0)


def _mlstm_chunk(carry, inp):
    c, n, m = carry
    q, k, v, ig, lf = inp
    L = q.shape[2]
    b = jnp.cumsum(lf, axis=-1)
    causal = jnp.tril(jnp.ones((L, L), bool))
    dmat = jnp.where(causal, b[..., :, None] - b[..., None, :] + ig[..., None, :], -jnp.inf)
    inter = b + m[..., None]
    m_t = jnp.maximum(inter, jnp.max(dmat, axis=-1))
    w_intra = jnp.exp(dmat - m_t[..., None])
    w_inter = jnp.exp(inter - m_t)
    s = jnp.einsum('bhtd,bhsd->bhts', q, k) * w_intra
    num = jnp.einsum('bhts,bhsv->bhtv', s, v) + w_inter[..., None] * jnp.einsum('bhtd,bhdv->bhtv', q, c)
    den = jnp.sum(s, axis=-1) + w_inter * jnp.einsum('bhtd,bhd->bht', q, n)
    h = num / jnp.maximum(jnp.abs(den), jnp.exp(-m_t))[..., None]
    b_last = b[..., -1]
    g = b_last[..., None] - b + ig
    m_new = jnp.maximum(b_last + m, jnp.max(g, axis=-1))
    wk = jnp.exp(g - m_new[..., None])
    decay = jnp.exp(b_last + m - m_new)
    c_new = decay[..., None, None] * c + jnp.einsum('bhs,bhsd,bhsv->bhdv', wk, k, v)
    n_new = decay[..., None] * n + jnp.einsum('bhs,bhsd->bhd', wk, k)
    return (c_new, n_new, m_new), h


def _to_chunks(a, nc, L):
    a = a.astype(F32).reshape((a.shape[0], nc, L) + a.shape[2:])
    return jnp.moveaxis(jnp.moveaxis(a, 3, 2), 1, 0)


def _mlstm_mix(q, k, v, ig, lf, c0, n0, m0):
    B, T, H, _ = q.shape
    L = M_CHUNK if T % M_CHUNK == 0 else T
    nc = T // L
    xs = (_to_chunks(q, nc, L), _to_chunks(k, nc, L), _to_chunks(v, nc, L),
          _to_chunks(ig, nc, L), _to_chunks(lf, nc, L))
    init = (c0.astype(F32), n0.astype(F32), m0.astype(F32))
    (c1, n1, m1), hs = lax.scan(_mlstm_chunk, init, xs)
    h = jnp.moveaxis(jnp.moveaxis(hs, 0, 1), 2, 3).reshape(B, T, H, -1)
    return h, c1, n1, m1


def _compress_blocks(x_raw, pe, w1, w2):
    B, Tk, G, d = x_raw.shape
    r = CMP_LEN // CMP_STRIDE
    n_sub = Tk // CMP_STRIDE
    n_cmp = n_sub - r + 1
    sub = x_raw[:, :n_sub * CMP_STRIDE].reshape(B, n_sub, CMP_STRIDE, G, d)
    sub = jnp.moveaxis(sub, 3, 2).reshape(B, n_sub, G, CMP_STRIDE * d)
    parts = jnp.einsum('bngx,rxh->rbngh', sub, w1.reshape(r, CMP_STRIDE * d, -1))
    pre = parts[0, :, :n_cmp]
    for j in range(1, r):
        pre = pre + parts[j, :, j:j + n_cmp]
    pre = pre + pe.reshape(-1) @ w1
    return jax.nn.gelu(pre) @ w2


def _nsa_attend(q, q_pos0, kc, vc, ks, vs, kw, vw, kw_pos0, gates):
    B, Tq = q.shape[:2]
    QB = NSA_QBLK if Tq % NSA_QBLK == 0 else Tq
    nqb = Tq // QB
    scale = N_DQK ** -0.5
    n_cmp = kc.shape[1]
    cmp_start = jnp.arange(n_cmp) * CMP_STRIDE
    cmp_end = cmp_start + CMP_LEN - 1
    Tk = ks.shape[1]
    n_sel = -(-Tk // SEL_LEN)
    pad = n_sel * SEL_LEN - Tk
    ksb = jnp.moveaxis(jnp.pad(ks, ((0, 0), (0, pad), (0, 0), (0, 0))).reshape(B, n_sel, SEL_LEN, N_KV, N_DQK), 3, 1)
    vsb = jnp.moveaxis(jnp.pad(vs, ((0, 0), (0, pad), (0, 0), (0, 0))).reshape(B, n_sel, SEL_LEN, N_KV, N_DV), 3, 1)
    top = min(SEL_TOPK, n_sel)
    sel_start = jnp.arange(n_sel) * SEL_LEN
    overlap = ((cmp_start[:, None] <= sel_start[None, :] + SEL_LEN - 1) &
               (cmp_end[:, None] >= sel_start[None, :])).astype(F32)
    kwp = jnp.pad(kw, ((0, 0), (WINDOW, 0), (0, 0), (0, 0)))
    vwp = jnp.pad(vw, ((0, 0), (WINDOW, 0), (0, 0), (0, 0)))
    WK = WINDOW + QB
    qg = q.reshape(B, Tq, N_KV, N_HPG, N_DQK)
    gg = gates.reshape(B, Tq, N_KV, N_HPG, 3)
    gather_blocks = jax.vmap(jax.vmap(lambda blk, ix: blk[ix]))
    jj = jnp.arange(n_sel)

    def block(j):
        q0 = j * QB
        qb = lax.dynamic_slice_in_dim(qg, q0, QB, axis=1)
        gb = lax.dynamic_slice_in_dim(gg, q0, QB, axis=1)
        qpos = q_pos0 + q0 + jnp.arange(QB)
        s_c = jnp.einsum('bqghd,bngd->bghqn', qb, kc).astype(F32) * scale
        p_c = _masked_softmax(s_c, cmp_end[None, :] <= qpos[:, None])
        o_c = jnp.einsum('bghqn,bngd->bqghd', p_c.astype(vc.dtype), vc)
        imp = jnp.einsum('bghqn,nj->bgqj', p_c, overlap)
        cur = qpos // SEL_LEN
        valid = sel_start[None, :] <= qpos[:, None]
        forced = (jj[None, :] == 0) | ((jj[None, :] <= cur[:, None]) & (jj[None, :] > cur[:, None] - N_LOCAL))
        rank = jnp.where(forced, jnp.inf, jnp.where(valid, imp, -jnp.inf))
        _, idx = lax.top_k(rank, top)
        kg = gather_blocks(ksb, idx)
        vg = gather_blocks(vsb, idx).reshape(B, N_KV, QB, top * SEL_LEN, N_DV)
        kpos = (idx[..., None] * SEL_LEN + jnp.arange(SEL_LEN)).reshape(B, N_KV, QB, top * SEL_LEN)
        s_s = jnp.einsum('bqghd,bgqtkd->bghqtk', qb, kg).astype(F32) * scale
        s_s = s_s.reshape(B, N_KV, N_HPG, QB, top * SEL_LEN)
        p_s = _masked_softmax(s_s, (kpos <= qpos[None, None, :, None])[:, :, None])
        o_s = jnp.einsum('bghqx,bgqxd->bqghd', p_s.astype(vg.dtype), vg)
        start = q_pos0 + q0 - kw_pos0
        kbw = lax.dynamic_slice_in_dim(kwp, start, WK, axis=1)
        vbw = lax.dynamic_slice_in_dim(vwp, start, WK, axis=1)
        wpos = q_pos0 + q0 - WINDOW + jnp.arange(WK)
        wmask = ((wpos[None, :] <= qpos[:, None]) & (wpos[None, :] > qpos[:, None] - WINDOW) &
                 (wpos[None, :] >= kw_pos0))
        s_w = jnp.einsum('bqghd,bkgd->bghqk', qb, kbw).astype(F32) * scale
        p_w = _masked_softmax(s_w, wmask)
        o_w = jnp.einsum('bghqk,bkgd->bqghd', p_w.astype(vbw.dtype), vbw)
        out = gb[..., 0:1] * o_c + gb[..., 1:2] * o_s + gb[..., 2:3] * o_w
        return out.astype(q.dtype)

    outs = lax.map(block, jnp.arange(nqb))
    return jnp.moveaxis(outs, 0, 1).reshape(B, Tq, N_WIDTH)


def _mix_half(x, past_cmp, past_slc, win_buf, c0, n0, m0, pos0, prm, tm):
    B, T, _ = x.shape
    z = _matmul(x.reshape(B * T, D_MODEL), prm['norm_mix'], prm['w_in_bf'], tm=tm, norm=True)
    z = z[:, :N_IN].reshape(B, T, N_IN)
    mq, mk, mv, mo, mi, mf, nq, kvc, kvs, kvw, ng = jnp.split(z, np.cumsum(IN_SIZES)[:-1].tolist(), axis=-1)
    mq = mq.reshape(B, T, M_HEADS, M_DQK)
    mk = mk.reshape(B, T, M_HEADS, M_DQK) * (M_DQK ** -0.5)
    mv = mv.reshape(B, T, M_HEADS, M_DV)
    ig = _softcap(mi + prm['b_mlstm_i'], GATE_CAP)
    lf = jax.nn.log_sigmoid(_softcap(mf + prm['b_mlstm_f'], GATE_CAP))
    h, c1, n1, m1 = _mlstm_mix(mq, mk, mv, ig, lf, c0, n0, m0)
    h = _rmsnorm(h, prm['mlstm_out_norm']).reshape(B, T, M_WIDTH)
    y_m = h * jax.nn.sigmoid(mo)
    q = _rmsnorm(nq.reshape(B, T, N_HEADS, N_DQK), prm['q_norm'])
    kvc = kvc.reshape(B, T, N_KV, N_KVW)
    kvs = kvs.reshape(B, T, N_KV, N_KVW)
    kvs = jnp.concatenate([_rmsnorm(kvs[..., :N_DQK], prm['k_norm_slc']), kvs[..., N_DQK:]], axis=-1)
    kvw = kvw.reshape(B, T, N_KV, N_KVW)
    kvw = jnp.concatenate([_rmsnorm(kvw[..., :N_DQK], prm['k_norm_win']), kvw[..., N_DQK:]], axis=-1)
    all_c = kvc if past_cmp is None else jnp.concatenate([past_cmp, kvc], axis=1)
    all_s = kvs if past_slc is None else jnp.concatenate([past_slc, kvs], axis=1)
    all_w = kvw if win_buf is None else jnp.concatenate([win_buf, kvw], axis=1)
    win_pos0 = pos0 + T - all_w.shape[1]
    kc = _rmsnorm(_compress_blocks(all_c[..., :N_DQK], prm['cmp_pe_k'], prm['cmp_w1_k'], prm['cmp_w2_k']),
                  prm['k_norm_cmp'])
    vc = _compress_blocks(all_c[..., N_DQK:], prm['cmp_pe_v'], prm['cmp_w1_v'], prm['cmp_w2_v'])
    gates = jax.nn.sigmoid(ng).reshape(B, T, N_HEADS, 3)
    y_n = _nsa_attend(q, pos0, kc, vc, all_s[..., :N_DQK], all_s[..., N_DQK:],
                      all_w[..., :N_DQK], all_w[..., N_DQK:], win_pos0, gates)
    mixed = jnp.concatenate([y_m, y_n], axis=-1).reshape(B * T, D_MODEL)
    x2 = x.reshape(B * T, D_MODEL) + _matmul(mixed, None, prm['w_out_bf'], tm=tm, norm=False)
    keep = min(WINDOW, pos0 + T)
    return x2, kvc, kvs, all_w[:, all_w.shape[1] - keep:], c1, n1, m1


def _ple_half(x3, ple, prm, tm):
    gate = _matmul(x3, prm['norm_ple'], prm['w_ple_gate_bf'], tm=tm, norm=True)
    proj = _matmul(ple, None, prm['w_ple_proj_bf'], tm=tm, norm=False)
    return x3 + jax.nn.sigmoid(gate) * proj


def kernel(x_prompt, x_sample, p_prompt, p_sample, cache_kv_cmp, cache_kv_slc, cache_kv_win, state_mlstm_C, state_mlstm_n, state_mlstm_m, page_table, norm_mix, w_in, b_mlstm_i, b_mlstm_f, mlstm_out_norm, q_norm, k_norm_cmp, k_norm_slc, k_norm_win, cmp_pe_k, cmp_w1_k, cmp_w2_k, cmp_pe_v, cmp_w1_v, cmp_w2_v, w_out, norm_ffn, w_router, b_router, w_gate, b_gate, w_up, b_up, w_down, b_down, norm_ple, w_ple_gate, w_ple_proj):
    depth = w_in.shape[0]
    assert depth == 1
    l = 0
    bp, sp, _ = x_prompt.shape
    db, ds, _ = x_sample.shape
    n_past = page_table.shape[1] * cache_kv_cmp.shape[2]
    n_in_pad = _round_up(N_IN, MM_TN)
    prm = dict(norm_mix=norm_mix[l], b_mlstm_i=b_mlstm_i[l], b_mlstm_f=b_mlstm_f[l],
               mlstm_out_norm=mlstm_out_norm[l], q_norm=q_norm[l], k_norm_cmp=k_norm_cmp[l],
               k_norm_slc=k_norm_slc[l], k_norm_win=k_norm_win[l], cmp_pe_k=cmp_pe_k[l],
               cmp_w1_k=cmp_w1_k[l], cmp_w2_k=cmp_w2_k[l], cmp_pe_v=cmp_pe_v[l], cmp_w1_v=cmp_w1_v[l],
               cmp_w2_v=cmp_w2_v[l], norm_ffn=norm_ffn[l], w_router=w_router[l],
               b_router=b_router[l], w_gate=w_gate[l], b_gate=b_gate[l], w_up=w_up[l], b_up=b_up[l],
               w_down=w_down[l], b_down=b_down[l], norm_ple=norm_ple[l])
    prm['w_in_bf'] = jnp.pad(w_in[l].astype(BF16), ((0, 0), (0, n_in_pad - N_IN)))
    prm['w_out_bf'] = w_out[l].astype(BF16)
    prm['w_ple_gate_bf'] = w_ple_gate[l].astype(BF16)
    prm['w_ple_proj_bf'] = w_ple_proj[l].astype(BF16)

    tm_p, tm_s = 512, db * ds
    c0 = jnp.zeros((bp, M_HEADS, M_DQK, M_DV), F32)
    n0 = jnp.zeros((bp, M_HEADS, M_DQK), F32)
    m0 = jnp.zeros((bp, M_HEADS), F32)
    xp2, a1, a2, a3, a4, a5, a6 = _mix_half(x_prompt, None, None, None, c0, n0, m0, 0, prm, tm_p)
    past_c = cache_kv_cmp[l][page_table].reshape(db, n_past, N_KV, N_KVW)
    past_s = cache_kv_slc[l][page_table].reshape(db, n_past, N_KV, N_KVW)
    xs2, b1, b2, b3, b4, b5, b6 = _mix_half(x_sample, past_c, past_s, cache_kv_win[l], state_mlstm_C[l],
                                            state_mlstm_n[l], state_mlstm_m[l], n_past, prm, tm_s)

    fp, lp = _router(xp2, prm['norm_ffn'], prm['w_router'], prm['b_router'], tm=tm_p)
    fs, ls = _router(xs2, prm['norm_ffn'], prm['w_router'], prm['b_router'], tm=tm_s)
    y_moe = _moe_ffn(jnp.concatenate([fp, fs], axis=0), jnp.concatenate([lp, ls], axis=0), prm)
    xp3 = xp2 + y_moe[:bp * sp]
    xs3 = xs2 + y_moe[bp * sp:]

    y_p = _ple_half(xp3, p_prompt[l].reshape(bp * sp, -1), prm, tm_p).reshape(bp, sp, D_MODEL)
    y_s = _ple_half(xs3, p_sample[l].reshape(db * ds, -1), prm, tm_s).reshape(db, ds, D_MODEL)

    st = lambda a: a[None]
    return (y_p, y_s, st(a1), st(a2), st(a3), st(a4), st(a5), st(a6),
            st(b1), st(b2), st(b3), st(b4), st(b5), st(b6))
```

```python
import functools

import numpy as np
import jax
import jax.numpy as jnp
from jax import lax
from jax.experimental import pallas as pl
from jax.experimental.pallas import tpu as pltpu

D_MODEL = 4096
M_HEADS = 4
M_DV = D_MODEL // (2 * M_HEADS)
M_DQK = M_DV // 2
M_WIDTH = M_HEADS * M_DV
M_CHUNK = 64
GATE_CAP = 15.0
N_HEADS = 16
N_KV = 4
N_HPG = N_HEADS // N_KV
N_DV = D_MODEL // (2 * N_HEADS)
N_DQK = 192
N_KVW = N_DQK + N_DV
N_WIDTH = N_HEADS * N_DV
CMP_LEN = 32
CMP_STRIDE = 16
SEL_LEN = 64
SEL_TOPK = 16
N_LOCAL = 2
WINDOW = 512
NSA_QBLK = 64
N_EXPERTS = 32
TOP_K = 4
SWIGLU_LIMIT = 7.0
SWIGLU_ALPHA = 1.702
EPS = 1e-6

IN_SIZES = (M_HEADS * M_DQK, M_HEADS * M_DQK, M_WIDTH, M_WIDTH, M_HEADS, M_HEADS,
            N_HEADS * N_DQK, N_KV * N_KVW, N_KV * N_KVW, N_KV * N_KVW, 3 * N_HEADS)
N_IN = sum(IN_SIZES)
IN_OFFS = tuple(int(v) for v in np.cumsum((0,) + IN_SIZES))

VMEM_LIMIT_BYTES = 56 * 1024 * 1024
LANE = 128
MM_TN = 512
MOE_TM = 256
MOE_TN = 512

BF16 = jnp.bfloat16
F32 = jnp.float32
NEG_INF = float("-inf")


def _round_up(a, b):
    return (a + b - 1) // b * b


Q_PAD = 256
Z_Q0 = IN_OFFS[4]
Z_KVC0 = Z_Q0 + N_HEADS * Q_PAD
Z_KVS0 = Z_KVC0 + N_KV * N_KVW
Z_KVW0 = Z_KVS0 + N_KV * N_KVW
Z_G0 = Z_KVW0 + N_KV * N_KVW
N_Z = _round_up(Z_G0 + LANE, MM_TN)
KV_ROW = N_KV * N_KVW
KV_PERM = N_KV * Q_PAD + N_KV * N_DV


def _dot(a, b):
    return jnp.dot(a, b, preferred_element_type=F32)


def _dot_nt(a, b):
    return lax.dot_general(a, b, (((1,), (1,)), ((), ())), preferred_element_type=F32)


def _split_bf16(a):
    hi = a.astype(BF16)
    return hi, (a - hi.astype(F32)).astype(BF16)


def _mm_kernel(a_ref, g_ref, w_ref, o_ref, a_bf, *, norm):
    @pl.when(pl.program_id(1) == 0)
    def _():
        a = a_ref[...]
        if norm:
            ms = jnp.mean(a * a, axis=-1, keepdims=True)
            a = a * lax.rsqrt(ms + EPS) * g_ref[...]
        a_bf[...] = a.astype(BF16)

    o_ref[...] = _dot(a_bf[...], w_ref[...])


def _matmul(a, gain, w_bf, *, tm, tn=MM_TN, norm):
    m, k = a.shape
    n = w_bf.shape[1]
    assert m % tm == 0 and n % tn == 0 and w_bf.shape[0] == k
    g = (jnp.ones((k,), F32) if gain is None else gain.astype(F32)).reshape(1, k)
    return pl.pallas_call(
        functools.partial(_mm_kernel, norm=norm),
        grid=(m // tm, n // tn),
        in_specs=[pl.BlockSpec((tm, k), lambda i, j: (i, 0)),
                  pl.BlockSpec((1, k), lambda i, j: (0, 0)),
                  pl.BlockSpec((k, tn), lambda i, j: (0, j))],
        out_specs=pl.BlockSpec((tm, tn), lambda i, j: (i, j)),
        out_shape=jax.ShapeDtypeStruct((m, n), F32),
        scratch_shapes=[pltpu.VMEM((tm, k), BF16)],
        compiler_params=pltpu.CompilerParams(
            dimension_semantics=("parallel", "arbitrary"),
            vmem_limit_bytes=VMEM_LIMIT_BYTES),
        name="dense_proj",
    )(a, g, w_bf)


def _relayout_w_in(w):
    d = w.shape[0]
    o = IN_OFFS
    wb = w.astype(BF16)
    cols = [wb[:, :o[4]]]
    zq = jnp.zeros((d, Q_PAD - N_DQK), BF16)
    for h in range(N_HEADS):
        cols += [wb[:, o[6] + h * N_DQK:o[6] + (h + 1) * N_DQK], zq]
    cols += [wb[:, o[7]:o[10]], wb[:, o[4]:o[6]], wb[:, o[10]:o[11]]]
    n_used = Z_G0 + (o[6] - o[4]) + (o[11] - o[10])
    cols += [jnp.zeros((d, N_Z - n_used), BF16)]
    return jnp.concatenate(cols, axis=1)


def _kv_prep_kernel(x_ref, gain_ref, vmask_ref, ind_ref, indt_ref, perm_ref, rows_ref, k_ref, v_ref):
    x = x_ref[...]
    sq_hi, sq_lo = _split_bf16(x * x)
    ss = _dot(sq_hi, ind_ref[...]) + _dot(sq_lo, ind_ref[...])
    inv_hi, inv_lo = _split_bf16(lax.rsqrt(ss * (1.0 / N_DQK) + EPS))
    scale = _dot(inv_hi, indt_ref[...]) + _dot(inv_lo, indt_ref[...]) + vmask_ref[...]
    y = x * scale * gain_ref[...]
    rows_ref[...] = y
    kv = _dot(y.astype(BF16), perm_ref[...])
    for g in range(N_KV):
        k_ref[g] = kv[:, g * Q_PAD:(g + 1) * Q_PAD].astype(BF16)
        v_ref[g] = kv[:, N_KV * Q_PAD + g * N_DV:N_KV * Q_PAD + (g + 1) * N_DV].astype(BF16)


def _kv_layout_constants():
    ind = np.zeros((KV_ROW, LANE), np.float32)
    vmask = np.zeros((1, KV_ROW), np.float32)
    perm = np.zeros((KV_ROW, KV_PERM), np.float32)
    for g in range(N_KV):
        ind[g * N_KVW:g * N_KVW + N_DQK, g] = 1.0
        vmask[0, g * N_KVW + N_DQK:(g + 1) * N_KVW] = 1.0
        for c in range(N_DQK):
            perm[g * N_KVW + c, g * Q_PAD + c] = 1.0
        for c in range(N_DV):
            perm[g * N_KVW + N_DQK + c, N_KV * Q_PAD + g * N_DV + c] = 1.0
    return ind, vmask, perm


def _kv_prep(z, col_block, k_gain, *, tm):
    rows = z.shape[0]
    ind, vmask, perm = _kv_layout_constants()
    gain_row = jnp.concatenate([k_gain.astype(F32), jnp.ones((N_DV,), F32)])
    gain_map = jnp.tile(gain_row, N_KV).reshape(1, KV_ROW)
    const = lambda shape: pl.BlockSpec(shape, lambda i: (0, 0))
    return pl.pallas_call(
        _kv_prep_kernel,
        grid=(rows // tm,),
        in_specs=[pl.BlockSpec((tm, KV_ROW), lambda i: (i, col_block)),
                  const((1, KV_ROW)), const((1, KV_ROW)), const((KV_ROW, LANE)),
                  const((LANE, KV_ROW)), const((KV_ROW, KV_PERM))],
        out_specs=[pl.BlockSpec((tm, KV_ROW), lambda i: (i, 0)),
                   pl.BlockSpec((N_KV, tm, Q_PAD), lambda i: (0, i, 0)),
                   pl.BlockSpec((N_KV, tm, N_DV), lambda i: (0, i, 0))],
        out_shape=[jax.ShapeDtypeStruct((rows, KV_ROW), F32),
                   jax.ShapeDtypeStruct((N_KV, rows, Q_PAD), BF16),
                   jax.ShapeDtypeStruct((N_KV, rows, N_DV), BF16)],
        compiler_params=pltpu.CompilerParams(
            dimension_semantics=("parallel",),
            vmem_limit_bytes=VMEM_LIMIT_BYTES),
        name="kv_prep",
    )(z, gain_map, jnp.asarray(vmask), jnp.asarray(ind, BF16), jnp.asarray(ind.T.copy(), BF16),
      jnp.asarray(perm, BF16))


def _exp_parts(s):
    mx = jnp.max(s, axis=-1, keepdims=True)
    mx = jnp.where(jnp.abs(mx) < jnp.inf, mx, 0.0)
    e = jnp.exp(s - mx)
    den = jnp.sum(e, axis=-1, keepdims=True)
    return e, jnp.where(den > 0, den, 1.0)


def _nsa_fresh_kernel(zq_ref, gt_ref, ks_ref, vs_ref, kw_ref, vw_ref, kc_ref, vc_ref,
                      qg_ref, ov_ref, ex_ref, o_ref, *, t_len, top, n_win):
    qb = NSA_QBLK
    q0 = pl.program_id(2) * qb
    scale = N_DQK ** -0.5
    n_sel = t_len // SEL_LEN

    parts = []
    for h in range(N_HPG):
        qh = zq_ref[:, h * Q_PAD:(h + 1) * Q_PAD]
        ms = jnp.sum(qh * qh, axis=-1, keepdims=True) * (1.0 / N_DQK)
        parts.append((qh * lax.rsqrt(ms + EPS) * qg_ref[...]).astype(BF16))
    qn = jnp.concatenate(parts, axis=0)
    qpos1 = q0 + lax.broadcasted_iota(jnp.int32, (qb, 1), 0)
    qpos = jnp.concatenate([qpos1] * N_HPG, axis=0)

    nc_pad = kc_ref.shape[1]
    cmp_end = lax.broadcasted_iota(jnp.int32, (1, nc_pad), 1) * CMP_STRIDE + (CMP_LEN - 1)
    s_c = jnp.where(cmp_end <= qpos, _dot_nt(qn, kc_ref[0]) * scale, NEG_INF)
    e_c, den_c = _exp_parts(s_c)
    p_c = e_c / den_c
    o_c = _dot(p_c.astype(BF16), vc_ref[0])

    psum = p_c[0:qb]
    for h in range(1, N_HPG):
        psum = psum + p_c[h * qb:(h + 1) * qb]
    ps_hi, ps_lo = _split_bf16(psum)
    imp = _dot(ps_hi, ov_ref[...]) + _dot(ps_lo, ov_ref[...])
    jl = lax.broadcasted_iota(jnp.int32, (1, LANE), 1)
    cur = qpos1 // SEL_LEN
    valid = jl * SEL_LEN <= qpos1
    forced = jnp.logical_or(jl == 0, jnp.logical_and(jl <= cur, jl > cur - N_LOCAL))
    rank = jnp.where(forced, jnp.inf, jnp.where(valid, imp, NEG_INF))
    ahead = jnp.zeros((qb, LANE), F32)
    for i in range(n_sel):
        ri = rank[:, i:i + 1]
        before = jnp.logical_or(ri > rank, jnp.logical_and(ri == rank, jl > i))
        ahead = ahead + jnp.where(before, 1.0, 0.0)
    sel = jnp.where(jnp.logical_and(ahead < top, valid), 1.0, 0.0).astype(BF16)
    selx = _dot(sel, ex_ref[...])
    kpos = lax.broadcasted_iota(jnp.int32, (1, t_len), 1)
    bias1 = jnp.where(jnp.logical_and(selx > 0.5, kpos <= qpos1), 0.0, NEG_INF)
    bias_s = jnp.concatenate([bias1] * N_HPG, axis=0)

    e_s, den_s = _exp_parts(_dot_nt(qn, ks_ref[0]) * scale + bias_s)
    o_s = _dot(e_s.astype(BF16), vs_ref[0]) / den_s

    ws = pl.multiple_of(jnp.maximum(q0 - WINDOW, 0), qb)
    wpos = ws + lax.broadcasted_iota(jnp.int32, (1, n_win), 1)
    mask_w = jnp.logical_and(wpos <= qpos, wpos > qpos - WINDOW)
    s_w = jnp.where(mask_w, _dot_nt(qn, kw_ref[0, pl.ds(ws, n_win), :]) * scale, NEG_INF)
    e_w, den_w = _exp_parts(s_w)
    o_w = _dot(e_w.astype(BF16), vw_ref[0, pl.ds(ws, n_win), :]) / den_w

    gt = jax.nn.sigmoid(gt_ref[0])
    for h in range(N_HPG):
        r = slice(h * qb, (h + 1) * qb)
        o_ref[:, h * N_DV:(h + 1) * N_DV] = (gt[:, 3 * h:3 * h + 1] * o_c[r]
                                             + gt[:, 3 * h + 1:3 * h + 2] * o_s[r]
                                             + gt[:, 3 * h + 2:3 * h + 3] * o_w[r])


def _nsa_fresh(z, graw, ks, vs, kw, vw, kc, vc, q_gain, *, batch, t_len, top=SEL_TOPK):
    qb = NSA_QBLK
    assert t_len % qb == 0 and t_len % SEL_LEN == 0 and SEL_LEN == qb
    nqb = t_len // qb
    n_sel = t_len // SEL_LEN
    n_cmp = t_len // CMP_STRIDE - CMP_LEN // CMP_STRIDE + 1
    nc_pad = kc.shape[1] // batch
    n_win = min(WINDOW + qb, t_len)
    cmp_start = np.arange(nc_pad) * CMP_STRIDE
    sel_start = np.arange(LANE) * SEL_LEN
    ov = ((cmp_start[:, None] <= sel_start[None, :] + SEL_LEN - 1) &
          (cmp_start[:, None] + CMP_LEN - 1 >= sel_start[None, :]) &
          (np.arange(nc_pad)[:, None] < n_cmp) & (np.arange(LANE)[None, :] < n_sel))
    ex = (np.arange(t_len)[None, :] // SEL_LEN) == np.arange(LANE)[:, None]
    qg = jnp.concatenate([q_gain.astype(F32), jnp.zeros((Q_PAD - N_DQK,), F32)]).reshape(1, Q_PAD)
    qw = N_HPG * Q_PAD
    return pl.pallas_call(
        functools.partial(_nsa_fresh_kernel, t_len=t_len, top=min(top, n_sel), n_win=n_win),
        grid=(batch, N_KV, nqb),
        in_specs=[pl.BlockSpec((qb, qw), lambda b, g, j: (b * nqb + j, Z_Q0 // qw + g)),
                  pl.BlockSpec((1, qb, LANE), lambda b, g, j: (g, b * nqb + j, 0)),
                  pl.BlockSpec((1, t_len, Q_PAD), lambda b, g, j: (g, b, 0)),
                  pl.BlockSpec((1, t_len, N_DV), lambda b, g, j: (g, b, 0)),
                  pl.BlockSpec((1, t_len, Q_PAD), lambda b, g, j: (g, b, 0)),
                  pl.BlockSpec((1, t_len, N_DV), lambda b, g, j: (g, b, 0)),
                  pl.BlockSpec((1, nc_pad, Q_PAD), lambda b, g, j: (g, b, 0)),
                  pl.BlockSpec((1, nc_pad, N_DV), lambda b, g, j: (g, b, 0)),
                  pl.BlockSpec((1, Q_PAD), lambda b, g, j: (0, 0)),
                  pl.BlockSpec((nc_pad, LANE), lambda b, g, j: (0, 0)),
                  pl.BlockSpec((LANE, t_len), lambda b, g, j: (0, 0))],
        out_specs=pl.BlockSpec((qb, N_HPG * N_DV), lambda b, g, j: (b * nqb + j, g)),
        out_shape=jax.ShapeDtypeStruct((batch * t_len, N_WIDTH), F32),
        compiler_params=pltpu.CompilerParams(
            dimension_semantics=("parallel", "parallel", "arbitrary"),
            vmem_limit_bytes=VMEM_LIMIT_BYTES),
        name="nsa_fresh",
    )(z, graw, ks, vs, kw, vw, kc, vc, qg, jnp.asarray(ov, BF16), jnp.asarray(ex, BF16))


def _router_kernel(x_ref, g_ref, whi_ref, wlo_ref, b_ref, f_ref, l_ref):
    x = x_ref[...]
    ms = jnp.mean(x * x, axis=-1, keepdims=True)
    f = x * lax.rsqrt(ms + EPS) * g_ref[...]
    f_hi, f_lo = _split_bf16(f)
    f_ref[...] = f_hi
    acc = _dot(f_hi, whi_ref[...]) + _dot(f_lo, whi_ref[...]) + _dot(f_hi, wlo_ref[...])
    l_ref[...] = acc + b_ref[...]


def _router(x, gain, w_router, b_router, *, tm):
    m, k = x.shape
    e = w_router.shape[1]
    w_hi = w_router.astype(BF16)
    w_lo = (w_router - w_hi.astype(F32)).astype(BF16)
    return pl.pallas_call(
        _router_kernel,
        grid=(m // tm,),
        in_specs=[pl.BlockSpec((tm, k), lambda i: (i, 0)),
                  pl.BlockSpec((1, k), lambda i: (0, 0)),
                  pl.BlockSpec((k, e), lambda i: (0, 0)),
                  pl.BlockSpec((k, e), lambda i: (0, 0)),
                  pl.BlockSpec((1, e), lambda i: (0, 0))],
        out_specs=[pl.BlockSpec((tm, k), lambda i: (i, 0)),
                   pl.BlockSpec((tm, e), lambda i: (i, 0))],
        out_shape=[jax.ShapeDtypeStruct((m, k), BF16),
                   jax.ShapeDtypeStruct((m, e), F32)],
        compiler_params=pltpu.CompilerParams(
            dimension_semantics=("parallel",),
            vmem_limit_bytes=VMEM_LIMIT_BYTES),
        name="moe_router",
    )(x, gain.reshape(1, k), w_hi, w_lo, b_router.reshape(1, e))


def _item_flags(blk_ref, exp_ref, st_ref, en_ref, tm):
    w = pl.program_id(1)
    e = exp_ref[w]
    b = blk_ref[w]
    prev = jnp.maximum(w - 1, 0)
    new_expert = jnp.logical_or(w == 0, exp_ref[prev] != e)
    new_block = jnp.logical_or(w == 0, blk_ref[prev] != b)
    rows = b * tm + lax.broadcasted_iota(jnp.int32, (tm, 1), 0)
    mask = jnp.logical_and(rows >= st_ref[e], rows < en_ref[e])
    return w, new_expert, new_block, mask


def _moe_up_kernel(blk_ref, exp_ref, st_ref, en_ref, tot_ref,
                   x_ref, wg_ref, wu_ref, bg_ref, bu_ref, h_ref, wg_bf, wu_bf, *, tm):
    w, new_expert, new_block, mask = _item_flags(blk_ref, exp_ref, st_ref, en_ref, tm)

    @pl.when(w < tot_ref[0])
    def _():
        @pl.when(new_expert)
        def _():
            wg_bf[...] = wg_ref[0].astype(BF16)
            wu_bf[...] = wu_ref[0].astype(BF16)

        x = x_ref[...]
        g = _dot(x, wg_bf[...]) + bg_ref[0]
        u = _dot(x, wu_bf[...]) + bu_ref[0]
        g = jnp.minimum(g, SWIGLU_LIMIT)
        u = jnp.clip(u, -SWIGLU_LIMIT, SWIGLU_LIMIT)
        hdn = (g * jax.nn.sigmoid(SWIGLU_ALPHA * g) * (u + 1.0)).astype(h_ref.dtype)
        keep = jnp.where(new_block, jnp.zeros_like(hdn), h_ref[...])
        h_ref[...] = jnp.where(mask, hdn, keep)


def _moe_down_kernel(blk_ref, exp_ref, st_ref, en_ref, tot_ref,
                     h_ref, wd_ref, bd_ref, y_ref, wd_bf, *, tm):
    w, new_expert, new_block, mask = _item_flags(blk_ref, exp_ref, st_ref, en_ref, tm)

    @pl.when(w < tot_ref[0])
    def _():
        @pl.when(new_expert)
        def _():
            wd_bf[...] = wd_ref[0].astype(BF16)

        y = _dot(h_ref[...], wd_bf[...]) + bd_ref[0]
        keep = jnp.where(new_block, jnp.zeros_like(y), y_ref[...])
        y_ref[...] = jnp.where(mask, y, keep)


def _moe_experts(x_sorted, sched, w_gate, b_gate, w_up, b_up, w_down, b_down):
    blk, exp, starts, ends, total = sched
    rows, d = x_sorted.shape
    n_e, _, d_h = w_gate.shape
    tm, tn = MOE_TM, MOE_TN
    n_items = blk.shape[0]
    params = pltpu.CompilerParams(dimension_semantics=("arbitrary", "arbitrary"),
                                  vmem_limit_bytes=VMEM_LIMIT_BYTES)

    row_spec = pl.BlockSpec((tm, d), lambda j, w, blk, exp, st, en, tot: (blk[w], 0))
    w_spec = pl.BlockSpec((1, d, tn), lambda j, w, blk, exp, st, en, tot: (exp[w], 0, j))
    b_spec = pl.BlockSpec((1, 1, tn), lambda j, w, blk, exp, st, en, tot: (exp[w], 0, j))
    out_spec = pl.BlockSpec((tm, tn), lambda j, w, blk, exp, st, en, tot: (blk[w], j))

    hidden = pl.pallas_call(
        functools.partial(_moe_up_kernel, tm=tm),
        grid_spec=pltpu.PrefetchScalarGridSpec(
            num_scalar_prefetch=5,
            grid=(d_h // tn, n_items),
            in_specs=[row_spec, w_spec, w_spec, b_spec, b_spec],
            out_specs=out_spec,
            scratch_shapes=[pltpu.VMEM((d, tn), BF16), pltpu.VMEM((d, tn), BF16)]),
        out_shape=jax.ShapeDtypeStruct((rows, d_h), BF16),
        compiler_params=params,
        name="moe_up",
    )(blk, exp, starts, ends, total, x_sorted, w_gate, w_up,
      b_gate.reshape(n_e, 1, d_h), b_up.reshape(n_e, 1, d_h))

    return pl.pallas_call(
        functools.partial(_moe_down_kernel, tm=tm),
        grid_spec=pltpu.PrefetchScalarGridSpec(
            num_scalar_prefetch=5,
            grid=(d // tn, n_items),
            in_specs=[row_spec, w_spec, b_spec],
            out_specs=out_spec,
            scratch_shapes=[pltpu.VMEM((d_h, tn), BF16)]),
        out_shape=jax.ShapeDtypeStruct((rows, d), F32),
        compiler_params=params,
        name="moe_down",
    )(blk, exp, starts, ends, total, hidden, w_down, b_down.reshape(n_e, 1, d))


def _moe_schedule(top_e, tm):
    nk = top_e.size
    assert nk % tm == 0
    n_blocks = nk // tm
    n_items = n_blocks + N_EXPERTS - 1
    flat_e = top_e.reshape(nk)
    order = jnp.argsort(flat_e)
    counts = jnp.bincount(flat_e, length=N_EXPERTS).astype(jnp.int32)
    ends = jnp.cumsum(counts).astype(jnp.int32)
    starts = ends - counts
    first_blk = starts // tm
    last_blk = jnp.where(counts > 0, (ends - 1) // tm, first_blk - 1)
    n_it = last_blk - first_blk + 1
    it_end = jnp.cumsum(n_it).astype(jnp.int32)
    it_start = it_end - n_it
    total = it_end[-1]
    w = jnp.arange(n_items, dtype=jnp.int32)
    w_c = jnp.minimum(w, total - 1)
    e_w = jnp.minimum(jnp.searchsorted(it_end, w_c, side='right'), N_EXPERTS - 1).astype(jnp.int32)
    blk_w = (first_blk[e_w] + (w_c - it_start[e_w])).astype(jnp.int32)
    slot_of = jnp.zeros((nk,), jnp.int32).at[order].set(jnp.arange(nk, dtype=jnp.int32))
    return order, slot_of, (blk_w, e_w, starts, ends, total.reshape(1))


def _moe_ffn(f_bf, logits, prm):
    n, d = f_bf.shape
    top_logit, top_e = lax.top_k(logits, TOP_K)
    gate = jax.nn.softmax(top_logit, axis=-1)
    order, slot_of, sched = _moe_schedule(top_e, MOE_TM)
    x_sorted = f_bf[order // TOP_K]
    y_sorted = _moe_experts(x_sorted, sched, prm['w_gate'], prm['b_gate'], prm['w_up'], prm['b_up'],
                            prm['w_down'], prm['b_down'])
    return jnp.einsum('nkd,nk->nd', y_sorted[slot_of].reshape(n, TOP_K, d), gate)


def _rmsnorm(x, g):
    xf = x.astype(F32)
    y = xf * lax.rsqrt(jnp.mean(xf * xf, axis=-1, keepdims=True) + EPS)
    return (y * g.astype(F32)).astype(x.dtype)


def _softcap(x, cap):
    return cap * jnp.tanh(x / cap)


def _masked_softmax(s, mask):
    s = jnp.where(mask, s.astype(F32), -jnp.inf)
    mx = jnp.max(s, axis=-1, keepdims=True)
    e = jnp.exp(s - jnp.where(jnp.isfinite(mx), mx, 0.0))
    den = jnp.sum(e, axis=-1, keepdims=True)
    return e / jnp.where(den > 0, den, 1.0)


def _mlstm_chunk(carry, inp):
    c, n, m = carry
    q, k, v, ig, lf = inp
    L = q.shape[2]
    b = jnp.cumsum(lf, axis=-1)
    causal = jnp.tril(jnp.ones((L, L), bool))
    dmat = jnp.where(causal, b[..., :, None] - b[..., None, :] + ig[..., None, :], -jnp.inf)
    inter = b + m[..., None]
    m_t = jnp.maximum(inter, jnp.max(dmat, axis=-1))
    w_intra = jnp.exp(dmat - m_t[..., None])
    w_inter = jnp.exp(inter - m_t)
    s = jnp.einsum('bhtd,bhsd->bhts', q, k) * w_intra
    num = jnp.einsum('bhts,bhsv->bhtv', s, v) + w_inter[..., None] * jnp.einsum('bhtd,bhdv->bhtv', q, c)
    den = jnp.sum(s, axis=-1) + w_inter * jnp.einsum('bhtd,bhd->bht', q, n)
    h = num / jnp.maximum(jnp.abs(den), jnp.exp(-m_t))[..., None]
    b_last = b[..., -1]
    g = b_last[..., None] - b + ig
    m_new = jnp.maximum(b_last + m, jnp.max(g, axis=-1))
    wk = jnp.exp(g - m_new[..., None])
    decay = jnp.exp(b_last + m - m_new)
    c_new = decay[..., None, None] * c + jnp.einsum('bhs,bhsd,bhsv->bhdv', wk, k, v)
    n_new = decay[..., None] * n + jnp.einsum('bhs,bhsd->bhd', wk, k)
    return (c_new, n_new, m_new), h


def _to_chunks(a, nc, L):
    a = a.astype(F32).reshape((a.shape[0], nc, L) + a.shape[2:])
    return jnp.moveaxis(jnp.moveaxis(a, 3, 2), 1, 0)


def _mlstm_mix(q, k, v, ig, lf, c0, n0, m0):
    B, T, H, _ = q.shape
    L = M_CHUNK if T % M_CHUNK == 0 else T
    nc = T // L
    xs = (_to_chunks(q, nc, L), _to_chunks(k, nc, L), _to_chunks(v, nc, L),
          _to_chunks(ig, nc, L), _to_chunks(lf, nc, L))
    init = (c0.astype(F32), n0.astype(F32), m0.astype(F32))
    (c1, n1, m1), hs = lax.scan(_mlstm_chunk, init, xs)
    h = jnp.moveaxis(jnp.moveaxis(hs, 0, 1), 2, 3).reshape(B, T, H, -1)
    return h, c1, n1, m1


def _compress_blocks(x_raw, pe, w1, w2):
    B, Tk, G, d = x_raw.shape
    r = CMP_LEN // CMP_STRIDE
    n_sub = Tk // CMP_STRIDE
    n_cmp = n_sub - r + 1
    sub = x_raw[:, :n_sub * CMP_STRIDE].reshape(B, n_sub, CMP_STRIDE, G, d)
    sub = jnp.moveaxis(sub, 3, 2).reshape(B, n_sub, G, CMP_STRIDE * d)
    parts = jnp.einsum('bngx,rxh->rbngh', sub, w1.reshape(r, CMP_STRIDE * d, -1))
    pre = parts[0, :, :n_cmp]
    for j in range(1, r):
        pre = pre + parts[j, :, j:j + n_cmp]
    pre = pre + pe.reshape(-1) @ w1
    return jax.nn.gelu(pre) @ w2


def _nsa_attend(q, q_pos0, kc, vc, ks, vs, kw, vw, kw_pos0, gates):
    B, Tq = q.shape[:2]
    QB = NSA_QBLK if Tq % NSA_QBLK == 0 else Tq
    nqb = Tq // QB
    scale = N_DQK ** -0.5
    n_cmp = kc.shape[1]
    cmp_start = jnp.arange(n_cmp) * CMP_STRIDE
    cmp_end = cmp_start + CMP_LEN - 1
    Tk = ks.shape[1]
    n_sel = -(-Tk // SEL_LEN)
    pad = n_sel * SEL_LEN - Tk
    ksb = jnp.moveaxis(jnp.pad(ks, ((0, 0), (0, pad), (0, 0), (0, 0))).reshape(B, n_sel, SEL_LEN, N_KV, N_DQK), 3, 1)
    vsb = jnp.moveaxis(jnp.pad(vs, ((0, 0), (0, pad), (0, 0), (0, 0))).reshape(B, n_sel, SEL_LEN, N_KV, N_DV), 3, 1)
    top = min(SEL_TOPK, n_sel)
    sel_start = jnp.arange(n_sel) * SEL_LEN
    overlap = ((cmp_start[:, None] <= sel_start[None, :] + SEL_LEN - 1) &
               (cmp_end[:, None] >= sel_start[None, :])).astype(F32)
    kwp = jnp.pad(kw, ((0, 0), (WINDOW, 0), (0, 0), (0, 0)))
    vwp = jnp.pad(vw, ((0, 0), (WINDOW, 0), (0, 0), (0, 0)))
    WK = WINDOW + QB
    qg = q.reshape(B, Tq, N_KV, N_HPG, N_DQK)
    gg = gates.reshape(B, Tq, N_KV, N_HPG, 3)
    gather_blocks = jax.vmap(jax.vmap(lambda blk, ix: blk[ix]))
    jj = jnp.arange(n_sel)

    def block(j):
        q0 = j * QB
        qb = lax.dynamic_slice_in_dim(qg, q0, QB, axis=1)
        gb = lax.dynamic_slice_in_dim(gg, q0, QB, axis=1)
        qpos = q_pos0 + q0 + jnp.arange(QB)
        s_c = jnp.einsum('bqghd,bngd->bghqn', qb, kc).astype(F32) * scale
        p_c = _masked_softmax(s_c, cmp_end[None, :] <= qpos[:, None])
        o_c = jnp.einsum('bghqn,bngd->bqghd', p_c.astype(vc.dtype), vc)
        imp = jnp.einsum('bghqn,nj->bgqj', p_c, overlap)
        cur = qpos // SEL_LEN
        valid = sel_start[None, :] <= qpos[:, None]
        forced = (jj[None, :] == 0) | ((jj[None, :] <= cur[:, None]) & (jj[None, :] > cur[:, None] - N_LOCAL))
        rank = jnp.where(forced, jnp.inf, jnp.where(valid, imp, -jnp.inf))
        _, idx = lax.top_k(rank, top)
        kg = gather_blocks(ksb, idx)
        vg = gather_blocks(vsb, idx).reshape(B, N_KV, QB, top * SEL_LEN, N_DV)
        kpos = (idx[..., None] * SEL_LEN + jnp.arange(SEL_LEN)).reshape(B, N_KV, QB, top * SEL_LEN)
        s_s = jnp.einsum('bqghd,bgqtkd->bghqtk', qb, kg).astype(F32) * scale
        s_s = s_s.reshape(B, N_KV, N_HPG, QB, top * SEL_LEN)
        p_s = _masked_softmax(s_s, (kpos <= qpos[None, None, :, None])[:, :, None])
        o_s = jnp.einsum('bghqx,bgqxd->bqghd', p_s.astype(vg.dtype), vg)
        start = q_pos0 + q0 - kw_pos0
        kbw = lax.dynamic_slice_in_dim(kwp, start, WK, axis=1)
        vbw = lax.dynamic_slice_in_dim(vwp, start, WK, axis=1)
        wpos = q_pos0 + q0 - WINDOW + jnp.arange(WK)
        wmask = ((wpos[None, :] <= qpos[:, None]) & (wpos[None, :] > qpos[:, None] - WINDOW) &
                 (wpos[None, :] >= kw_pos0))
        s_w = jnp.einsum('bqghd,bkgd->bghqk', qb, kbw).astype(F32) * scale
        p_w = _masked_softmax(s_w, wmask)
        o_w = jnp.einsum('bghqk,bkgd->bqghd', p_w.astype(vbw.dtype), vbw)
        out = gb[..., 0:1] * o_c + gb[..., 1:2] * o_s + gb[..., 2:3] * o_w
        return out.astype(q.dtype)

    outs = lax.map(block, jnp.arange(nqb))
    return jnp.moveaxis(outs, 0, 1).reshape(B, Tq, N_WIDTH)


def _mlstm_half(z, c0, n0, m0, prm, B, T):
    o = IN_OFFS
    mq = z[:, o[0]:o[1]].reshape(B, T, M_HEADS, M_DQK)
    mk = z[:, o[1]:o[2]].reshape(B, T, M_HEADS, M_DQK) * (M_DQK ** -0.5)
    mv = z[:, o[2]:o[3]].reshape(B, T, M_HEADS, M_DV)
    mo = z[:, o[3]:o[4]].reshape(B, T, M_WIDTH)
    mi = z[:, Z_G0:Z_G0 + M_HEADS].reshape(B, T, M_HEADS)
    mf = z[:, Z_G0 + M_HEADS:Z_G0 + 2 * M_HEADS].reshape(B, T, M_HEADS)
    ig = _softcap(mi + prm['b_mlstm_i'], GATE_CAP)
    lf = jax.nn.log_sigmoid(_softcap(mf + prm['b_mlstm_f'], GATE_CAP))
    h, c1, n1, m1 = _mlstm_mix(mq, mk, mv, ig, lf, c0, n0, m0)
    h = _rmsnorm(h, prm['mlstm_out_norm']).reshape(B, T, M_WIDTH)
    return h * jax.nn.sigmoid(mo), c1, n1, m1


def _gate_cols(z):
    return z[:, Z_G0 + 2 * M_HEADS:Z_G0 + 2 * M_HEADS + 3 * N_HEADS]


def _mix_fresh(x, prm, tm):
    B, T, _ = x.shape
    rows = B * T
    z = _matmul(x.reshape(rows, D_MODEL), prm['norm_mix'], prm['w_in_bf'], tm=tm, norm=True)
    c0 = jnp.zeros((B, M_HEADS, M_DQK, M_DV), F32)
    n0 = jnp.zeros((B, M_HEADS, M_DQK), F32)
    m0 = jnp.zeros((B, M_HEADS), F32)
    y_m, c1, n1, m1 = _mlstm_half(z, c0, n0, m0, prm, B, T)

    kvc = z[:, Z_KVC0:Z_KVC0 + KV_ROW].reshape(B, T, N_KV, N_KVW)
    kvs_rows, ks, vs = _kv_prep(z, Z_KVS0 // KV_ROW, prm['k_norm_slc'], tm=tm)
    kvw_rows, kw, vw = _kv_prep(z, Z_KVW0 // KV_ROW, prm['k_norm_win'], tm=tm)
    kc = _rmsnorm(_compress_blocks(kvc[..., :N_DQK], prm['cmp_pe_k'], prm['cmp_w1_k'], prm['cmp_w2_k']),
                  prm['k_norm_cmp'])
    vc = _compress_blocks(kvc[..., N_DQK:], prm['cmp_pe_v'], prm['cmp_w1_v'], prm['cmp_w2_v'])
    n_cmp = kc.shape[1]
    nc_pad = _round_up(n_cmp, LANE)
    kc_g = jnp.pad(jnp.moveaxis(kc, 2, 0), ((0, 0), (0, 0), (0, nc_pad - n_cmp), (0, Q_PAD - N_DQK)))
    vc_g = jnp.pad(jnp.moveaxis(vc, 2, 0), ((0, 0), (0, 0), (0, nc_pad - n_cmp), (0, 0)))
    kc_g = kc_g.astype(BF16).reshape(N_KV, B * nc_pad, Q_PAD)
    vc_g = vc_g.astype(BF16).reshape(N_KV, B * nc_pad, N_DV)
    graw = jnp.moveaxis(_gate_cols(z).reshape(rows, N_KV, 3 * N_HPG), 1, 0)
    graw = jnp.pad(graw, ((0, 0), (0, 0), (0, LANE - 3 * N_HPG)))
    y_n = _nsa_fresh(z, graw, ks, vs, kw, vw, kc_g, vc_g, prm['q_norm'], batch=B, t_len=T)

    mixed = jnp.concatenate([y_m.reshape(rows, M_WIDTH), y_n], axis=-1)
    x2 = x.reshape(rows, D_MODEL) + _matmul(mixed, None, prm['w_out_bf'], tm=tm, norm=False)
    keep = min(WINDOW, T)
    kv_win = kvw_rows.reshape(B, T, N_KV, N_KVW)[:, T - keep:]
    return x2, kvc, kvs_rows.reshape(B, T, N_KV, N_KVW), kv_win, c1, n1, m1


def _mix_cached(x, past_cmp, past_slc, win_buf, c0, n0, m0, pos0, prm, tm):
    B, T, _ = x.shape
    rows = B * T
    z = _matmul(x.reshape(rows, D_MODEL), prm['norm_mix'], prm['w_in_bf'], tm=tm, norm=True)
    y_m, c1, n1, m1 = _mlstm_half(z, c0, n0, m0, prm, B, T)
    nq = z[:, Z_Q0:Z_KVC0].reshape(B, T, N_HEADS, Q_PAD)[..., :N_DQK]
    q = _rmsnorm(nq, prm['q_norm'])
    kvc = z[:, Z_KVC0:Z_KVC0 + KV_ROW].reshape(B, T, N_KV, N_KVW)
    kvs = z[:, Z_KVS0:Z_KVS0 + KV_ROW].reshape(B, T, N_KV, N_KVW)
    kvs = jnp.concatenate([_rmsnorm(kvs[..., :N_DQK], prm['k_norm_slc']), kvs[..., N_DQK:]], axis=-1)
    kvw = z[:, Z_KVW0:Z_KVW0 + KV_ROW].reshape(B, T, N_KV, N_KVW)
    kvw = jnp.concatenate([_rmsnorm(kvw[..., :N_DQK], prm['k_norm_win']), kvw[..., N_DQK:]], axis=-1)
    all_c = jnp.concatenate([past_cmp, kvc], axis=1)
    all_s = jnp.concatenate([past_slc, kvs], axis=1)
    all_w = jnp.concatenate([win_buf, kvw], axis=1)
    win_pos0 = pos0 + T - all_w.shape[1]
    kc = _rmsnorm(_compress_blocks(all_c[..., :N_DQK], prm['cmp_pe_k'], prm['cmp_w1_k'], prm['cmp_w2_k']),
                  prm['k_norm_cmp'])
    vc = _compress_blocks(all_c[..., N_DQK:], prm['cmp_pe_v'], prm['cmp_w1_v'], prm['cmp_w2_v'])
    gates = jax.nn.sigmoid(_gate_cols(z)).reshape(B, T, N_HEADS, 3)
    y_n = _nsa_attend(q, pos0, kc, vc, all_s[..., :N_DQK], all_s[..., N_DQK:],
                      all_w[..., :N_DQK], all_w[..., N_DQK:], win_pos0, gates)
    mixed = jnp.concatenate([y_m, y_n], axis=-1).reshape(rows, D_MODEL)
    x2 = x.reshape(rows, D_MODEL) + _matmul(mixed, None, prm['w_out_bf'], tm=tm, norm=False)
    keep = min(WINDOW, pos0 + T)
    return x2, kvc, kvs, all_w[:, all_w.shape[1] - keep:], c1, n1, m1


def _ple_half(x3, ple, prm, tm):
    gate = _matmul(x3, prm['norm_ple'], prm['w_ple_gate_bf'], tm=tm, norm=True)
    proj = _matmul(ple, None, prm['w_ple_proj_bf'], tm=tm, norm=False)
    return x3 + jax.nn.sigmoid(gate) * proj


def kernel(x_prompt, x_sample, p_prompt, p_sample, cache_kv_cmp, cache_kv_slc, cache_kv_win, state_mlstm_C, state_mlstm_n, state_mlstm_m, page_table, norm_mix, w_in, b_mlstm_i, b_mlstm_f, mlstm_out_norm, q_norm, k_norm_cmp, k_norm_slc, k_norm_win, cmp_pe_k, cmp_w1_k, cmp_w2_k, cmp_pe_v, cmp_w1_v, cmp_w2_v, w_out, norm_ffn, w_router, b_router, w_gate, b_gate, w_up, b_up, w_down, b_down, norm_ple, w_ple_gate, w_ple_proj):
    depth = w_in.shape[0]
    assert depth == 1
    l = 0
    bp, sp, _ = x_prompt.shape
    db, ds, _ = x_sample.shape
    n_past = page_table.shape[1] * cache_kv_cmp.shape[2]
    prm = dict(norm_mix=norm_mix[l], b_mlstm_i=b_mlstm_i[l], b_mlstm_f=b_mlstm_f[l],
               mlstm_out_norm=mlstm_out_norm[l], q_norm=q_norm[l], k_norm_cmp=k_norm_cmp[l],
               k_norm_slc=k_norm_slc[l], k_norm_win=k_norm_win[l], cmp_pe_k=cmp_pe_k[l],
               cmp_w1_k=cmp_w1_k[l], cmp_w2_k=cmp_w2_k[l], cmp_pe_v=cmp_pe_v[l], cmp_w1_v=cmp_w1_v[l],
               cmp_w2_v=cmp_w2_v[l], norm_ffn=norm_ffn[l], w_router=w_router[l],
               b_router=b_router[l], w_gate=w_gate[l], b_gate=b_gate[l], w_up=w_up[l], b_up=b_up[l],
               w_down=w_down[l], b_down=b_down[l], norm_ple=norm_ple[l])
    prm['w_in_bf'] = _relayout_w_in(w_in[l])
    prm['w_out_bf'] = w_out[l].astype(BF16)
    prm['w_ple_gate_bf'] = w_ple_gate[l].astype(BF16)
    prm['w_ple_proj_bf'] = w_ple_proj[l].astype(BF16)

    tm_p, tm_s = 512, db * ds
    xp2, a1, a2, a3, a4, a5, a6 = _mix_fresh(x_prompt, prm, tm_p)
    past_c = cache_kv_cmp[l][page_table].reshape(db, n_past, N_KV, N_KVW)
    past_s = cache_kv_slc[l][page_table].reshape(db, n_past, N_KV, N_KVW)
    xs2, b1, b2, b3, b4, b5, b6 = _mix_cached(x_sample, past_c, past_s, cache_kv_win[l], state_mlstm_C[l],
                                              state_mlstm_n[l], state_mlstm_m[l], n_past, prm, tm_s)

    fp, lp = _router(xp2, prm['norm_ffn'], prm['w_router'], prm['b_router'], tm=tm_p)
    fs, ls = _router(xs2, prm['norm_ffn'], prm['w_router'], prm['b_router'], tm=tm_s)
    y_moe = _moe_ffn(jnp.concatenate([fp, fs], axis=0), jnp.concatenate([lp, ls], axis=0), prm)
    xp3 = xp2 + y_moe[:bp * sp]
    xs3 = xs2 + y_moe[bp * sp:]

    y_p = _ple_half(xp3, p_prompt[l].reshape(bp * sp, -1), prm, tm_p).reshape(bp, sp, D_MODEL)
    y_s = _ple_half(xs3, p_sample[l].reshape(db * ds, -1), prm, tm_s).reshape(db, ds, D_MODEL)

    st = lambda a: a[None]
    return (y_p, y_s, st(a1), st(a2), st(a3), st(a4), st(a5), st(a6),
            st(b1), st(b2), st(b3), st(b4), st(b5), st(b6))
```

```python
import functools

import numpy as np
import jax
import jax.numpy as jnp
from jax import lax
from jax.experimental import pallas as pl
from jax.experimental.pallas import tpu as pltpu

D_MODEL = 4096
M_HEADS = 4
M_DV = D_MODEL // (2 * M_HEADS)
M_DQK = M_DV // 2
M_WIDTH = M_HEADS * M_DV
M_CHUNK = 64
GATE_CAP = 15.0
N_HEADS = 16
N_KV = 4
N_HPG = N_HEADS // N_KV
N_DV = D_MODEL // (2 * N_HEADS)
N_DQK = 192
N_KVW = N_DQK + N_DV
N_WIDTH = N_HEADS * N_DV
CMP_LEN = 32
CMP_STRIDE = 16
SEL_LEN = 64
SEL_TOPK = 16
N_LOCAL = 2
WINDOW = 512
NSA_QBLK = 64
N_EXPERTS = 32
TOP_K = 4
SWIGLU_LIMIT = 7.0
SWIGLU_ALPHA = 1.702
EPS = 1e-6

IN_SIZES = (M_HEADS * M_DQK, M_HEADS * M_DQK, M_WIDTH, M_WIDTH, M_HEADS, M_HEADS,
            N_HEADS * N_DQK, N_KV * N_KVW, N_KV * N_KVW, N_KV * N_KVW, 3 * N_HEADS)
N_IN = sum(IN_SIZES)
IN_OFFS = tuple(int(v) for v in np.cumsum((0,) + IN_SIZES))

VMEM_LIMIT_BYTES = 56 * 1024 * 1024
LANE = 128
MM_TN = 512
MOE_TM = 256
MOE_TN = 512

BF16 = jnp.bfloat16
F32 = jnp.float32
NEG_INF = float("-inf")


def _round_up(a, b):
    return (a + b - 1) // b * b


Q_PAD = 256
Z_Q0 = IN_OFFS[4]
Z_KVC0 = Z_Q0 + N_HEADS * Q_PAD
Z_KVS0 = Z_KVC0 + N_KV * N_KVW
Z_KVW0 = Z_KVS0 + N_KV * N_KVW
Z_G0 = Z_KVW0 + N_KV * N_KVW
N_Z = _round_up(Z_G0 + LANE, MM_TN)
KV_ROW = N_KV * N_KVW
KV_PERM = N_KV * Q_PAD + N_KV * N_DV


def _dot(a, b):
    return jnp.dot(a, b, preferred_element_type=F32)


def _dot_nt(a, b):
    return lax.dot_general(a, b, (((1,), (1,)), ((), ())), preferred_element_type=F32)


def _split_bf16(a):
    hi = a.astype(BF16)
    return hi, (a - hi.astype(F32)).astype(BF16)


def _mm_kernel(a_ref, g_ref, w_ref, o_ref, a_bf, *, norm):
    @pl.when(pl.program_id(1) == 0)
    def _():
        a = a_ref[...]
        if norm:
            ms = jnp.mean(a * a, axis=-1, keepdims=True)
            a = a * lax.rsqrt(ms + EPS) * g_ref[...]
        a_bf[...] = a.astype(BF16)

    o_ref[...] = _dot(a_bf[...], w_ref[...])


def _matmul(a, gain, w_bf, *, tm, tn=MM_TN, norm):
    m, k = a.shape
    n = w_bf.shape[1]
    assert m % tm == 0 and n % tn == 0 and w_bf.shape[0] == k
    g = (jnp.ones((k,), F32) if gain is None else gain.astype(F32)).reshape(1, k)
    return pl.pallas_call(
        functools.partial(_mm_kernel, norm=norm),
        grid=(m // tm, n // tn),
        in_specs=[pl.BlockSpec((tm, k), lambda i, j: (i, 0)),
                  pl.BlockSpec((1, k), lambda i, j: (0, 0)),
                  pl.BlockSpec((k, tn), lambda i, j: (0, j))],
        out_specs=pl.BlockSpec((tm, tn), lambda i, j: (i, j)),
        out_shape=jax.ShapeDtypeStruct((m, n), F32),
        scratch_shapes=[pltpu.VMEM((tm, k), BF16)],
        compiler_params=pltpu.CompilerParams(
            dimension_semantics=("parallel", "arbitrary"),
            vmem_limit_bytes=VMEM_LIMIT_BYTES),
        name="dense_proj",
    )(a, g, w_bf)


def _relayout_w_in(w):
    d = w.shape[0]
    o = IN_OFFS
    wb = w.astype(BF16)
    cols = [wb[:, :o[4]]]
    zq = jnp.zeros((d, Q_PAD - N_DQK), BF16)
    for h in range(N_HEADS):
        cols += [wb[:, o[6] + h * N_DQK:o[6] + (h + 1) * N_DQK], zq]
    cols += [wb[:, o[7]:o[10]], wb[:, o[4]:o[6]], wb[:, o[10]:o[11]]]
    n_used = Z_G0 + (o[6] - o[4]) + (o[11] - o[10])
    cols += [jnp.zeros((d, N_Z - n_used), BF16)]
    return jnp.concatenate(cols, axis=1)


def _kv_prep_kernel(x_ref, gain_ref, vmask_ref, ind_ref, indt_ref, perm_ref, rows_ref, k_ref, v_ref):
    x = x_ref[...]
    sq_hi, sq_lo = _split_bf16(x * x)
    ss = _dot(sq_hi, ind_ref[...]) + _dot(sq_lo, ind_ref[...])
    inv_hi, inv_lo = _split_bf16(lax.rsqrt(ss * (1.0 / N_DQK) + EPS))
    scale = _dot(inv_hi, indt_ref[...]) + _dot(inv_lo, indt_ref[...]) + vmask_ref[...]
    y = x * scale * gain_ref[...]
    rows_ref[...] = y
    kv = _dot(y.astype(BF16), perm_ref[...])
    for g in range(N_KV):
        k_ref[g] = kv[:, g * Q_PAD:(g + 1) * Q_PAD].astype(BF16)
        v_ref[g] = kv[:, N_KV * Q_PAD + g * N_DV:N_KV * Q_PAD + (g + 1) * N_DV].astype(BF16)


def _kv_layout_constants():
    ind = np.zeros((KV_ROW, LANE), np.float32)
    vmask = np.zeros((1, KV_ROW), np.float32)
    perm = np.zeros((KV_ROW, KV_PERM), np.float32)
    for g in range(N_KV):
        ind[g * N_KVW:g * N_KVW + N_DQK, g] = 1.0
        vmask[0, g * N_KVW + N_DQK:(g + 1) * N_KVW] = 1.0
        for c in range(N_DQK):
            perm[g * N_KVW + c, g * Q_PAD + c] = 1.0
        for c in range(N_DV):
            perm[g * N_KVW + N_DQK + c, N_KV * Q_PAD + g * N_DV + c] = 1.0
    return ind, vmask, perm


def _kv_prep(z, col_block, k_gain, *, tm):
    rows = z.shape[0]
    ind, vmask, perm = _kv_layout_constants()
    gain_row = jnp.concatenate([k_gain.astype(F32), jnp.ones((N_DV,), F32)])
    gain_map = jnp.tile(gain_row, N_KV).reshape(1, KV_ROW)
    const = lambda shape: pl.BlockSpec(shape, lambda i: (0, 0))
    return pl.pallas_call(
        _kv_prep_kernel,
        grid=(rows // tm,),
        in_specs=[pl.BlockSpec((tm, KV_ROW), lambda i: (i, col_block)),
                  const((1, KV_ROW)), const((1, KV_ROW)), const((KV_ROW, LANE)),
                  const((LANE, KV_ROW)), const((KV_ROW, KV_PERM))],
        out_specs=[pl.BlockSpec((tm, KV_ROW), lambda i: (i, 0)),
                   pl.BlockSpec((N_KV, tm, Q_PAD), lambda i: (0, i, 0)),
                   pl.BlockSpec((N_KV, tm, N_DV), lambda i: (0, i, 0))],
        out_shape=[jax.ShapeDtypeStruct((rows, KV_ROW), F32),
                   jax.ShapeDtypeStruct((N_KV, rows, Q_PAD), BF16),
                   jax.ShapeDtypeStruct((N_KV, rows, N_DV), BF16)],
        compiler_params=pltpu.CompilerParams(
            dimension_semantics=("parallel",),
            vmem_limit_bytes=VMEM_LIMIT_BYTES),
        name="kv_prep",
    )(z, gain_map, jnp.asarray(vmask), jnp.asarray(ind, BF16), jnp.asarray(ind.T.copy(), BF16),
      jnp.asarray(perm, BF16))


def _exp_parts(s):
    mx = jnp.max(s, axis=-1, keepdims=True)
    mx = jnp.where(jnp.abs(mx) < jnp.inf, mx, 0.0)
    e = jnp.exp(s - mx)
    den = jnp.sum(e, axis=-1, keepdims=True)
    return e, jnp.where(den > 0, den, 1.0)


def _nsa_fresh_kernel(zq_ref, gt_ref, ks_ref, vs_ref, kw_ref, vw_ref, kc_ref, vc_ref,
                      qg_ref, ov_ref, ex_ref, o_ref, *, t_len, top, n_win):
    qb = NSA_QBLK
    q0 = pl.program_id(2) * qb
    scale = N_DQK ** -0.5
    n_sel = t_len // SEL_LEN

    parts = []
    for h in range(N_HPG):
        qh = zq_ref[:, h * Q_PAD:(h + 1) * Q_PAD]
        ms = jnp.sum(qh * qh, axis=-1, keepdims=True) * (1.0 / N_DQK)
        parts.append((qh * lax.rsqrt(ms + EPS) * qg_ref[...]).astype(BF16))
    qn = jnp.concatenate(parts, axis=0)
    qpos1 = q0 + lax.broadcasted_iota(jnp.int32, (qb, 1), 0)
    qpos = jnp.concatenate([qpos1] * N_HPG, axis=0)

    nc_pad = kc_ref.shape[1]
    cmp_end = lax.broadcasted_iota(jnp.int32, (1, nc_pad), 1) * CMP_STRIDE + (CMP_LEN - 1)
    s_c = jnp.where(cmp_end <= qpos, _dot_nt(qn, kc_ref[0]) * scale, NEG_INF)
    e_c, den_c = _exp_parts(s_c)
    p_c = e_c / den_c
    o_c = _dot(p_c.astype(BF16), vc_ref[0])

    psum = p_c[0:qb]
    for h in range(1, N_HPG):
        psum = psum + p_c[h * qb:(h + 1) * qb]
    ps_hi, ps_lo = _split_bf16(psum)
    imp = _dot(ps_hi, ov_ref[...]) + _dot(ps_lo, ov_ref[...])
    jl = lax.broadcasted_iota(jnp.int32, (1, LANE), 1)
    cur = qpos1 // SEL_LEN
    valid = jl * SEL_LEN <= qpos1
    forced = jnp.logical_or(jl == 0, jnp.logical_and(jl <= cur, jl > cur - N_LOCAL))
    rank = jnp.where(forced, jnp.inf, jnp.where(valid, imp, NEG_INF))
    ahead = jnp.zeros((qb, LANE), F32)
    for i in range(n_sel):
        ri = rank[:, i:i + 1]
        before = jnp.logical_or(ri > rank, jnp.logical_and(ri == rank, jl > i))
        ahead = ahead + jnp.where(before, 1.0, 0.0)
    sel = jnp.where(jnp.logical_and(ahead < top, valid), 1.0, 0.0).astype(BF16)
    selx = _dot(sel, ex_ref[...])
    kpos = lax.broadcasted_iota(jnp.int32, (1, t_len), 1)
    bias1 = jnp.where(jnp.logical_and(selx > 0.5, kpos <= qpos1), 0.0, NEG_INF)
    bias_s = jnp.concatenate([bias1] * N_HPG, axis=0)

    e_s, den_s = _exp_parts(_dot_nt(qn, ks_ref[0]) * scale + bias_s)
    o_s = _dot(e_s.astype(BF16), vs_ref[0]) / den_s

    ws = pl.multiple_of(jnp.maximum(q0 - WINDOW, 0), qb)
    wpos = ws + lax.broadcasted_iota(jnp.int32, (1, n_win), 1)
    mask_w = jnp.logical_and(wpos <= qpos, wpos > qpos - WINDOW)
    s_w = jnp.where(mask_w, _dot_nt(qn, kw_ref[0, pl.ds(ws, n_win), :]) * scale, NEG_INF)
    e_w, den_w = _exp_parts(s_w)
    o_w = _dot(e_w.astype(BF16), vw_ref[0, pl.ds(ws, n_win), :]) / den_w

    gt = jax.nn.sigmoid(gt_ref[0])
    for h in range(N_HPG):
        r = slice(h * qb, (h + 1) * qb)
        o_ref[:, h * N_DV:(h + 1) * N_DV] = (gt[:, 3 * h:3 * h + 1] * o_c[r]
                                             + gt[:, 3 * h + 1:3 * h + 2] * o_s[r]
                                             + gt[:, 3 * h + 2:3 * h + 3] * o_w[r])


def _nsa_fresh(z, graw, ks, vs, kw, vw, kc, vc, q_gain, *, batch, t_len, top=SEL_TOPK):
    qb = NSA_QBLK
    assert t_len % qb == 0 and t_len % SEL_LEN == 0 and SEL_LEN == qb
    nqb = t_len // qb
    n_sel = t_len // SEL_LEN
    n_cmp = t_len // CMP_STRIDE - CMP_LEN // CMP_STRIDE + 1
    nc_pad = kc.shape[1] // batch
    n_win = min(WINDOW + qb, t_len)
    cmp_start = np.arange(nc_pad) * CMP_STRIDE
    sel_start = np.arange(LANE) * SEL_LEN
    ov = ((cmp_start[:, None] <= sel_start[None, :] + SEL_LEN - 1) &
          (cmp_start[:, None] + CMP_LEN - 1 >= sel_start[None, :]) &
          (np.arange(nc_pad)[:, None] < n_cmp) & (np.arange(LANE)[None, :] < n_sel))
    ex = (np.arange(t_len)[None, :] // SEL_LEN) == np.arange(LANE)[:, None]
    qg = jnp.concatenate([q_gain.astype(F32), jnp.zeros((Q_PAD - N_DQK,), F32)]).reshape(1, Q_PAD)
    qw = N_HPG * Q_PAD
    return pl.pallas_call(
        functools.partial(_nsa_fresh_kernel, t_len=t_len, top=min(top, n_sel), n_win=n_win),
        grid=(batch, N_KV, nqb),
        in_specs=[pl.BlockSpec((qb, qw), lambda b, g, j: (b * nqb + j, Z_Q0 // qw + g)),
                  pl.BlockSpec((1, qb, LANE), lambda b, g, j: (g, b * nqb + j, 0)),
                  pl.BlockSpec((1, t_len, Q_PAD), lambda b, g, j: (g, b, 0)),
                  pl.BlockSpec((1, t_len, N_DV), lambda b, g, j: (g, b, 0)),
                  pl.BlockSpec((1, t_len, Q_PAD), lambda b, g, j: (g, b, 0)),
                  pl.BlockSpec((1, t_len, N_DV), lambda b, g, j: (g, b, 0)),
                  pl.BlockSpec((1, nc_pad, Q_PAD), lambda b, g, j: (g, b, 0)),
                  pl.BlockSpec((1, nc_pad, N_DV), lambda b, g, j: (g, b, 0)),
                  pl.BlockSpec((1, Q_PAD), lambda b, g, j: (0, 0)),
                  pl.BlockSpec((nc_pad, LANE), lambda b, g, j: (0, 0)),
                  pl.BlockSpec((LANE, t_len), lambda b, g, j: (0, 0))],
        out_specs=pl.BlockSpec((qb, N_HPG * N_DV), lambda b, g, j: (b * nqb + j, g)),
        out_shape=jax.ShapeDtypeStruct((batch * t_len, N_WIDTH), F32),
        compiler_params=pltpu.CompilerParams(
            dimension_semantics=("parallel", "parallel", "arbitrary"),
            vmem_limit_bytes=VMEM_LIMIT_BYTES),
        name="nsa_fresh",
    )(z, graw, ks, vs, kw, vw, kc, vc, qg, jnp.asarray(ov, BF16), jnp.asarray(ex, BF16))


PAGES_PER_STEP = 8
ROW_TILES = -(-N_KVW // LANE)


def _page_specs(page, n_pages):
    def spec(i):
        return pl.BlockSpec((1, page, N_KV, N_KVW),
                            lambda b, c, pt: (pt[b * n_pages + c * PAGES_PER_STEP + i], 0, 0, 0))
    return [spec(i) for i in range(PAGES_PER_STEP)]


def _cmp_partial_kernel(pt_ref, *refs, page):
    pages = refs[:PAGES_PER_STEP]
    wk0_ref, wk1_ref, wv1_ref, wv2_ref, pk_ref, pv_ref, stage = refs[PAGES_PER_STEP:]
    keys = PAGES_PER_STEP * page
    n_sub = keys // CMP_STRIDE
    tail = jnp.zeros((page, ROW_TILES * LANE - N_KVW), F32)
    for g in range(N_KV):
        for i, p in enumerate(pages):
            x = p[0, :, g, :]
            r0 = g * keys + i * page
            stage[0, r0:r0 + page, :] = x[:, :LANE]
            stage[1, r0:r0 + page, :] = x[:, LANE:2 * LANE]
            stage[2, r0:r0 + page, :] = jnp.concatenate([x[:, 2 * LANE:], tail], axis=-1)
    acc_k = jnp.zeros((N_KV * n_sub, 2 * N_DQK), F32)
    acc_v = jnp.zeros((N_KV * n_sub, 2 * N_DV), F32)
    for t in range(CMP_STRIDE):
        a0, a1, a2 = [stage[j, pl.ds(t, N_KV * n_sub, stride=CMP_STRIDE), :].astype(BF16)
                      for j in range(ROW_TILES)]
        acc_k = acc_k + _dot(a0, wk0_ref[t]) + _dot(a1, wk1_ref[t])
        acc_v = acc_v + _dot(a1, wv1_ref[t]) + _dot(a2, wv2_ref[t])
    for g in range(N_KV):
        pk_ref[0, g] = acc_k[g * n_sub:(g + 1) * n_sub]
        pv_ref[0, g] = acc_v[g * n_sub:(g + 1) * n_sub]


def _cmp_partial(cache, page_table, w1_k, w1_v):
    n_pool, page = cache.shape[:2]
    batch, n_pages = page_table.shape
    assert page % CMP_STRIDE == 0 and n_pages % PAGES_PER_STEP == 0 and CMP_LEN == 2 * CMP_STRIDE
    assert ROW_TILES == 3 and LANE < N_DQK <= 2 * LANE < N_KVW
    sub_step = PAGES_PER_STEP * page // CMP_STRIDE
    n_sub = n_pages * page // CMP_STRIDE

    def halves(w1, d):
        w = w1.astype(BF16).reshape(2, CMP_STRIDE, d, d)
        return jnp.concatenate([w[0], w[1]], axis=-1)

    wk, wv = halves(w1_k, N_DQK), halves(w1_v, N_DV)
    zrow = lambda n, w: jnp.zeros((CMP_STRIDE, n, w.shape[-1]), BF16)
    wk0 = wk[:, :LANE]
    wk1 = jnp.concatenate([wk[:, LANE:], zrow(2 * LANE - N_DQK, wk)], axis=1)
    wv1 = jnp.concatenate([zrow(N_DQK - LANE, wv), wv[:, :2 * LANE - N_DQK]], axis=1)
    wv2 = jnp.concatenate([wv[:, 2 * LANE - N_DQK:], zrow(ROW_TILES * LANE - N_KVW, wv)], axis=1)

    const = lambda w: pl.BlockSpec(w.shape, lambda b, c, pt: (0, 0, 0))
    out = lambda d: pl.BlockSpec((1, N_KV, sub_step, 2 * d), lambda b, c, pt: (b, 0, c, 0))
    return pl.pallas_call(
        functools.partial(_cmp_partial_kernel, page=page),
        grid_spec=pltpu.PrefetchScalarGridSpec(
            num_scalar_prefetch=1,
            grid=(batch, n_pages // PAGES_PER_STEP),
            in_specs=_page_specs(page, n_pages) + [const(wk0), const(wk1), const(wv1), const(wv2)],
            out_specs=[out(N_DQK), out(N_DV)],
            scratch_shapes=[pltpu.VMEM((ROW_TILES, N_KV * PAGES_PER_STEP * page, LANE), F32)]),
        out_shape=[jax.ShapeDtypeStruct((batch, N_KV, n_sub, 2 * N_DQK), F32),
                   jax.ShapeDtypeStruct((batch, N_KV, n_sub, 2 * N_DV), F32)],
        compiler_params=pltpu.CompilerParams(
            dimension_semantics=("parallel", "arbitrary"),
            vmem_limit_bytes=VMEM_LIMIT_BYTES),
        name="cmp_partial",
    )(page_table.reshape(-1), *([cache] * PAGES_PER_STEP), wk0, wk1, wv1, wv2)


def _group_rows(refs, g):
    return jnp.concatenate([r[0, :, g, :] for r in refs], axis=0).astype(BF16)


def _group_scores(qx, krows):
    rg = qx.shape[0] // N_KV
    s = [_dot_nt(qx[g * rg:(g + 1) * rg], krows[g]) for g in range(N_KV)]
    return jnp.concatenate(s, axis=0) * (N_DQK ** -0.5)


def _group_pv(p, krows):
    rg = p.shape[0] // N_KV
    return jnp.concatenate([_dot(p[g * rg:(g + 1) * rg], krows[g]) for g in range(N_KV)], axis=0)


def _online_update(s, krows, m_sc, l_sc, acc_sc):
    m_old = m_sc[:, :1]
    m_new = jnp.maximum(m_old, jnp.max(s, axis=-1, keepdims=True))
    m_safe = jnp.where(jnp.abs(m_new) < jnp.inf, m_new, 0.0)
    alpha = jnp.exp(m_old - m_safe)
    p = jnp.exp(s - m_safe)
    l_sc[...] = jnp.broadcast_to(alpha * l_sc[:, :1] + jnp.sum(p, axis=-1, keepdims=True), l_sc.shape)
    acc_sc[...] = alpha * acc_sc[...] + _group_pv(p.astype(BF16), krows)
    m_sc[...] = jnp.broadcast_to(m_new, m_sc.shape)


def _nsa_cached_kernel(pt_ref, q_ref, sel_ref, ex_ref, news_ref, win_ref, neww_ref, *refs, t_new):
    pages = refs[:PAGES_PER_STEP]
    os_ref, ow_ref, m_sc, l_sc, acc_sc = refs[PAGES_PER_STEP:]
    c = pl.program_id(1)
    qx = q_ref[0]
    n_q = qx.shape[0]

    @pl.when(c == 0)
    def _():
        m_sc[...] = jnp.full(m_sc.shape, NEG_INF, F32)
        l_sc[...] = jnp.zeros(l_sc.shape, F32)
        acc_sc[...] = jnp.zeros(acc_sc.shape, F32)

    krows = [_group_rows(pages, g) for g in range(N_KV)]
    selx = _dot(sel_ref[0, 0], ex_ref[...])
    s = jnp.where(selx > 0.5, _group_scores(qx, krows), NEG_INF)
    _online_update(s, krows, m_sc, l_sc, acc_sc)

    @pl.when(c == pl.num_programs(1) - 1)
    def _():
        qi = lax.broadcasted_iota(jnp.int32, (n_q, 1), 0) % t_new
        nrows = [_group_rows([news_ref], g) for g in range(N_KV)]
        ki = lax.broadcasted_iota(jnp.int32, (1, news_ref.shape[1]), 1)
        s_n = jnp.where(ki <= qi, _group_scores(qx, nrows), NEG_INF)
        _online_update(s_n, nrows, m_sc, l_sc, acc_sc)
        os_ref[0] = (acc_sc[...] / l_sc[:, :1])[:, N_DQK:]

        n_buf = win_ref.shape[1]
        wrows = [_group_rows([win_ref, neww_ref], g) for g in range(N_KV)]
        ri = lax.broadcasted_iota(jnp.int32, (1, n_buf + neww_ref.shape[1]), 1)
        vis = jnp.logical_or(jnp.logical_and(ri < n_buf, ri > qi + (n_buf - WINDOW)),
                             jnp.logical_and(ri >= n_buf, ri - n_buf <= qi))
        e_w, den_w = _exp_parts(jnp.where(vis, _group_scores(qx, wrows), NEG_INF))
        ow_ref[0] = (_group_pv(e_w.astype(BF16), wrows) / den_w)[:, N_DQK:]


def _nsa_cached(cache, page_table, qx, sel, new_slc, win_buf, new_win, *, t_new):
    page = cache.shape[1]
    batch, n_pages = page_table.shape
    n_q = qx.shape[1]
    keys_step = PAGES_PER_STEP * page
    assert keys_step % SEL_LEN == 0 and keys_step // SEL_LEN <= LANE and n_pages % PAGES_PER_STEP == 0
    assert win_buf.shape[1] >= WINDOW and t_new <= new_slc.shape[1]
    ex = (np.arange(keys_step)[None, :] // SEL_LEN) == np.arange(LANE)[:, None]
    per_b = lambda a: pl.BlockSpec((1,) + a.shape[1:], lambda b, c, pt: (b,) + (0,) * (a.ndim - 1))
    out = pl.BlockSpec((1, n_q, N_DV), lambda b, c, pt: (b, 0, 0))
    return pl.pallas_call(
        functools.partial(_nsa_cached_kernel, t_new=t_new),
        grid_spec=pltpu.PrefetchScalarGridSpec(
            num_scalar_prefetch=1,
            grid=(batch, n_pages // PAGES_PER_STEP),
            in_specs=[per_b(qx),
                      pl.BlockSpec((1, 1, n_q, LANE), lambda b, c, pt: (b, c, 0, 0)),
                      pl.BlockSpec((LANE, keys_step), lambda b, c, pt: (0, 0)),
                      per_b(new_slc), per_b(win_buf), per_b(new_win)]
            + _page_specs(page, n_pages),
            out_specs=[out, out],
            scratch_shapes=[pltpu.VMEM((n_q, LANE), F32), pltpu.VMEM((n_q, LANE), F32),
                            pltpu.VMEM((n_q, N_KVW), F32)]),
        out_shape=[jax.ShapeDtypeStruct((batch, n_q, N_DV), F32)] * 2,
        compiler_params=pltpu.CompilerParams(
            dimension_semantics=("parallel", "arbitrary"),
            vmem_limit_bytes=VMEM_LIMIT_BYTES),
        name="nsa_cached",
    )(page_table.reshape(-1), qx, sel, jnp.asarray(ex, BF16), new_slc, win_buf, new_win,
      *([cache] * PAGES_PER_STEP))


def _router_kernel(x_ref, g_ref, whi_ref, wlo_ref, b_ref, f_ref, l_ref):
    x = x_ref[...]
    ms = jnp.mean(x * x, axis=-1, keepdims=True)
    f = x * lax.rsqrt(ms + EPS) * g_ref[...]
    f_hi, f_lo = _split_bf16(f)
    f_ref[...] = f_hi
    acc = _dot(f_hi, whi_ref[...]) + _dot(f_lo, whi_ref[...]) + _dot(f_hi, wlo_ref[...])
    l_ref[...] = acc + b_ref[...]


def _router(x, gain, w_router, b_router, *, tm):
    m, k = x.shape
    e = w_router.shape[1]
    w_hi = w_router.astype(BF16)
    w_lo = (w_router - w_hi.astype(F32)).astype(BF16)
    return pl.pallas_call(
        _router_kernel,
        grid=(m // tm,),
        in_specs=[pl.BlockSpec((tm, k), lambda i: (i, 0)),
                  pl.BlockSpec((1, k), lambda i: (0, 0)),
                  pl.BlockSpec((k, e), lambda i: (0, 0)),
                  pl.BlockSpec((k, e), lambda i: (0, 0)),
                  pl.BlockSpec((1, e), lambda i: (0, 0))],
        out_specs=[pl.BlockSpec((tm, k), lambda i: (i, 0)),
                   pl.BlockSpec((tm, e), lambda i: (i, 0))],
        out_shape=[jax.ShapeDtypeStruct((m, k), BF16),
                   jax.ShapeDtypeStruct((m, e), F32)],
        compiler_params=pltpu.CompilerParams(
            dimension_semantics=("parallel",),
            vmem_limit_bytes=VMEM_LIMIT_BYTES),
        name="moe_router",
    )(x, gain.reshape(1, k), w_hi, w_lo, b_router.reshape(1, e))


def _item_flags(blk_ref, exp_ref, st_ref, en_ref, tm):
    w = pl.program_id(1)
    e = exp_ref[w]
    b = blk_ref[w]
    prev = jnp.maximum(w - 1, 0)
    new_expert = jnp.logical_or(w == 0, exp_ref[prev] != e)
    new_block = jnp.logical_or(w == 0, blk_ref[prev] != b)
    rows = b * tm + lax.broadcasted_iota(jnp.int32, (tm, 1), 0)
    mask = jnp.logical_and(rows >= st_ref[e], rows < en_ref[e])
    return w, new_expert, new_block, mask


def _moe_up_kernel(blk_ref, exp_ref, st_ref, en_ref, tot_ref,
                   x_ref, wg_ref, wu_ref, bg_ref, bu_ref, h_ref, wg_bf, wu_bf, *, tm):
    w, new_expert, new_block, mask = _item_flags(blk_ref, exp_ref, st_ref, en_ref, tm)

    @pl.when(w < tot_ref[0])
    def _():
        @pl.when(new_expert)
        def _():
            wg_bf[...] = wg_ref[0].astype(BF16)
            wu_bf[...] = wu_ref[0].astype(BF16)

        x = x_ref[...]
        g = _dot(x, wg_bf[...]) + bg_ref[0]
        u = _dot(x, wu_bf[...]) + bu_ref[0]
        g = jnp.minimum(g, SWIGLU_LIMIT)
        u = jnp.clip(u, -SWIGLU_LIMIT, SWIGLU_LIMIT)
        hdn = (g * jax.nn.sigmoid(SWIGLU_ALPHA * g) * (u + 1.0)).astype(h_ref.dtype)
        keep = jnp.where(new_block, jnp.zeros_like(hdn), h_ref[...])
        h_ref[...] = jnp.where(mask, hdn, keep)


def _moe_down_kernel(blk_ref, exp_ref, st_ref, en_ref, tot_ref,
                     h_ref, wd_ref, bd_ref, y_ref, wd_bf, *, tm):
    w, new_expert, new_block, mask = _item_flags(blk_ref, exp_ref, st_ref, en_ref, tm)

    @pl.when(w < tot_ref[0])
    def _():
        @pl.when(new_expert)
        def _():
            wd_bf[...] = wd_ref[0].astype(BF16)

        y = _dot(h_ref[...], wd_bf[...]) + bd_ref[0]
        keep = jnp.where(new_block, jnp.zeros_like(y), y_ref[...])
        y_ref[...] = jnp.where(mask, y, keep)


def _moe_experts(x_sorted, sched, w_gate, b_gate, w_up, b_up, w_down, b_down):
    blk, exp, starts, ends, total = sched
    rows, d = x_sorted.shape
    n_e, _, d_h = w_gate.shape
    tm, tn = MOE_TM, MOE_TN
    n_items = blk.shape[0]
    params = pltpu.CompilerParams(dimension_semantics=("arbitrary", "arbitrary"),
                                  vmem_limit_bytes=VMEM_LIMIT_BYTES)

    row_spec = pl.BlockSpec((tm, d), lambda j, w, blk, exp, st, en, tot: (blk[w], 0))
    w_spec = pl.BlockSpec((1, d, tn), lambda j, w, blk, exp, st, en, tot: (exp[w], 0, j))
    b_spec = pl.BlockSpec((1, 1, tn), lambda j, w, blk, exp, st, en, tot: (exp[w], 0, j))
    out_spec = pl.BlockSpec((tm, tn), lambda j, w, blk, exp, st, en, tot: (blk[w], j))

    hidden = pl.pallas_call(
        functools.partial(_moe_up_kernel, tm=tm),
        grid_spec=pltpu.PrefetchScalarGridSpec(
            num_scalar_prefetch=5,
            grid=(d_h // tn, n_items),
            in_specs=[row_spec, w_spec, w_spec, b_spec, b_spec],
            out_specs=out_spec,
            scratch_shapes=[pltpu.VMEM((d, tn), BF16), pltpu.VMEM((d, tn), BF16)]),
        out_shape=jax.ShapeDtypeStruct((rows, d_h), BF16),
        compiler_params=params,
        name="moe_up",
    )(blk, exp, starts, ends, total, x_sorted, w_gate, w_up,
      b_gate.reshape(n_e, 1, d_h), b_up.reshape(n_e, 1, d_h))

    return pl.pallas_call(
        functools.partial(_moe_down_kernel, tm=tm),
        grid_spec=pltpu.PrefetchScalarGridSpec(
            num_scalar_prefetch=5,
            grid=(d // tn, n_items),
            in_specs=[row_spec, w_spec, b_spec],
            out_specs=out_spec,
            scratch_shapes=[pltpu.VMEM((d_h, tn), BF16)]),
        out_shape=jax.ShapeDtypeStruct((rows, d), F32),
        compiler_params=params,
        name="moe_down",
    )(blk, exp, starts, ends, total, hidden, w_down, b_down.reshape(n_e, 1, d))


def _moe_schedule(top_e, tm):
    nk = top_e.size
    assert nk % tm == 0
    n_blocks = nk // tm
    n_items = n_blocks + N_EXPERTS - 1
    flat_e = top_e.reshape(nk)
    order = jnp.argsort(flat_e)
    counts = jnp.bincount(flat_e, length=N_EXPERTS).astype(jnp.int32)
    ends = jnp.cumsum(counts).astype(jnp.int32)
    starts = ends - counts
    first_blk = starts // tm
    last_blk = jnp.where(counts > 0, (ends - 1) // tm, first_blk - 1)
    n_it = last_blk - first_blk + 1
    it_end = jnp.cumsum(n_it).astype(jnp.int32)
    it_start = it_end - n_it
    total = it_end[-1]
    w = jnp.arange(n_items, dtype=jnp.int32)
    w_c = jnp.minimum(w, total - 1)
    e_w = jnp.minimum(jnp.searchsorted(it_end, w_c, side='right'), N_EXPERTS - 1).astype(jnp.int32)
    blk_w = (first_blk[e_w] + (w_c - it_start[e_w])).astype(jnp.int32)
    slot_of = jnp.zeros((nk,), jnp.int32).at[order].set(jnp.arange(nk, dtype=jnp.int32))
    return order, slot_of, (blk_w, e_w, starts, ends, total.reshape(1))


def _moe_ffn(f_bf, logits, prm):
    n, d = f_bf.shape
    top_logit, top_e = lax.top_k(logits, TOP_K)
    gate = jax.nn.softmax(top_logit, axis=-1)
    order, slot_of, sched = _moe_schedule(top_e, MOE_TM)
    x_sorted = f_bf[order // TOP_K]
    y_sorted = _moe_experts(x_sorted, sched, prm['w_gate'], prm['b_gate'], prm['w_up'], prm['b_up'],
                            prm['w_down'], prm['b_down'])
    return jnp.einsum('nkd,nk->nd', y_sorted[slot_of].reshape(n, TOP_K, d), gate)


def _rmsnorm(x, g):
    xf = x.astype(F32)
    y = xf * lax.rsqrt(jnp.mean(xf * xf, axis=-1, keepdims=True) + EPS)
    return (y * g.astype(F32)).astype(x.dtype)


def _softcap(x, cap):
    return cap * jnp.tanh(x / cap)


def _masked_softmax(s, mask):
    s = jnp.where(mask, s.astype(F32), -jnp.inf)
    mx = jnp.max(s, axis=-1, keepdims=True)
    e = jnp.exp(s - jnp.where(jnp.isfinite(mx), mx, 0.0))
    den = jnp.sum(e, axis=-1, keepdims=True)
    return e / jnp.where(den > 0, den, 1.0)


def _mlstm_chunk(carry, inp):
    c, n, m = carry
    q, k, v, ig, lf = inp
    L = q.shape[2]
    b = jnp.cumsum(lf, axis=-1)
    causal = jnp.tril(jnp.ones((L, L), bool))
    dmat = jnp.where(causal, b[..., :, None] - b[..., None, :] + ig[..., None, :], -jnp.inf)
    inter = b + m[..., None]
    m_t = jnp.maximum(inter, jnp.max(dmat, axis=-1))
    w_intra = jnp.exp(dmat - m_t[..., None])
    w_inter = jnp.exp(inter - m_t)
    s = jnp.einsum('bhtd,bhsd->bhts', q, k) * w_intra
    num = jnp.einsum('bhts,bhsv->bhtv', s, v) + w_inter[..., None] * jnp.einsum('bhtd,bhdv->bhtv', q, c)
    den = jnp.sum(s, axis=-1) + w_inter * jnp.einsum('bhtd,bhd->bht', q, n)
    h = num / jnp.maximum(jnp.abs(den), jnp.exp(-m_t))[..., None]
    b_last = b[..., -1]
    g = b_last[..., None] - b + ig
    m_new = jnp.maximum(b_last + m, jnp.max(g, axis=-1))
    wk = jnp.exp(g - m_new[..., None])
    decay = jnp.exp(b_last + m - m_new)
    c_new = decay[..., None, None] * c + jnp.einsum('bhs,bhsd,bhsv->bhdv', wk, k, v)
    n_new = decay[..., None] * n + jnp.einsum('bhs,bhsd->bhd', wk, k)
    return (c_new, n_new, m_new), h


def _to_chunks(a, nc, L):
    a = a.astype(F32).reshape((a.shape[0], nc, L) + a.shape[2:])
    return jnp.moveaxis(jnp.moveaxis(a, 3, 2), 1, 0)


def _mlstm_mix(q, k, v, ig, lf, c0, n0, m0):
    B, T, H, _ = q.shape
    L = M_CHUNK if T % M_CHUNK == 0 else T
    nc = T // L
    xs = (_to_chunks(q, nc, L), _to_chunks(k, nc, L), _to_chunks(v, nc, L),
          _to_chunks(ig, nc, L), _to_chunks(lf, nc, L))
    init = (c0.astype(F32), n0.astype(F32), m0.astype(F32))
    (c1, n1, m1), hs = lax.scan(_mlstm_chunk, init, xs)
    h = jnp.moveaxis(jnp.moveaxis(hs, 0, 1), 2, 3).reshape(B, T, H, -1)
    return h, c1, n1, m1


def _compress_blocks(x_raw, pe, w1, w2):
    B, Tk, G, d = x_raw.shape
    r = CMP_LEN // CMP_STRIDE
    n_sub = Tk // CMP_STRIDE
    n_cmp = n_sub - r + 1
    sub = x_raw[:, :n_sub * CMP_STRIDE].reshape(B, n_sub, CMP_STRIDE, G, d)
    sub = jnp.moveaxis(sub, 3, 2).reshape(B, n_sub, G, CMP_STRIDE * d)
    parts = jnp.einsum('bngx,rxh->rbngh', sub, w1.reshape(r, CMP_STRIDE * d, -1))
    pre = parts[0, :, :n_cmp]
    for j in range(1, r):
        pre = pre + parts[j, :, j:j + n_cmp]
    pre = pre + pe.reshape(-1) @ w1
    return jax.nn.gelu(pre) @ w2


def _nsa_attend(q, q_pos0, kc, vc, ks, vs, kw, vw, kw_pos0, gates):
    B, Tq = q.shape[:2]
    QB = NSA_QBLK if Tq % NSA_QBLK == 0 else Tq
    nqb = Tq // QB
    scale = N_DQK ** -0.5
    n_cmp = kc.shape[1]
    cmp_start = jnp.arange(n_cmp) * CMP_STRIDE
    cmp_end = cmp_start + CMP_LEN - 1
    Tk = ks.shape[1]
    n_sel = -(-Tk // SEL_LEN)
    pad = n_sel * SEL_LEN - Tk
    ksb = jnp.moveaxis(jnp.pad(ks, ((0, 0), (0, pad), (0, 0), (0, 0))).reshape(B, n_sel, SEL_LEN, N_KV, N_DQK), 3, 1)
    vsb = jnp.moveaxis(jnp.pad(vs, ((0, 0), (0, pad), (0, 0), (0, 0))).reshape(B, n_sel, SEL_LEN, N_KV, N_DV), 3, 1)
    top = min(SEL_TOPK, n_sel)
    sel_start = jnp.arange(n_sel) * SEL_LEN
    overlap = ((cmp_start[:, None] <= sel_start[None, :] + SEL_LEN - 1) &
               (cmp_end[:, None] >= sel_start[None, :])).astype(F32)
    kwp = jnp.pad(kw, ((0, 0), (WINDOW, 0), (0, 0), (0, 0)))
    vwp = jnp.pad(vw, ((0, 0), (WINDOW, 0), (0, 0), (0, 0)))
    WK = WINDOW + QB
    qg = q.reshape(B, Tq, N_KV, N_HPG, N_DQK)
    gg = gates.reshape(B, Tq, N_KV, N_HPG, 3)
    gather_blocks = jax.vmap(jax.vmap(lambda blk, ix: blk[ix]))
    jj = jnp.arange(n_sel)

    def block(j):
        q0 = j * QB
        qb = lax.dynamic_slice_in_dim(qg, q0, QB, axis=1)
        gb = lax.dynamic_slice_in_dim(gg, q0, QB, axis=1)
        qpos = q_pos0 + q0 + jnp.arange(QB)
        s_c = jnp.einsum('bqghd,bngd->bghqn', qb, kc).astype(F32) * scale
        p_c = _masked_softmax(s_c, cmp_end[None, :] <= qpos[:, None])
        o_c = jnp.einsum('bghqn,bngd->bqghd', p_c.astype(vc.dtype), vc)
        imp = jnp.einsum('bghqn,nj->bgqj', p_c, overlap)
        cur = qpos // SEL_LEN
        valid = sel_start[None, :] <= qpos[:, None]
        forced = (jj[None, :] == 0) | ((jj[None, :] <= cur[:, None]) & (jj[None, :] > cur[:, None] - N_LOCAL))
        rank = jnp.where(forced, jnp.inf, jnp.where(valid, imp, -jnp.inf))
        _, idx = lax.top_k(rank, top)
        kg = gather_blocks(ksb, idx)
        vg = gather_blocks(vsb, idx).reshape(B, N_KV, QB, top * SEL_LEN, N_DV)
        kpos = (idx[..., None] * SEL_LEN + jnp.arange(SEL_LEN)).reshape(B, N_KV, QB, top * SEL_LEN)
        s_s = jnp.einsum('bqghd,bgqtkd->bghqtk', qb, kg).astype(F32) * scale
        s_s = s_s.reshape(B, N_KV, N_HPG, QB, top * SEL_LEN)
        p_s = _masked_softmax(s_s, (kpos <= qpos[None, None, :, None])[:, :, None])
        o_s = jnp.einsum('bghqx,bgqxd->bqghd', p_s.astype(vg.dtype), vg)
        start = q_pos0 + q0 - kw_pos0
        kbw = lax.dynamic_slice_in_dim(kwp, start, WK, axis=1)
        vbw = lax.dynamic_slice_in_dim(vwp, start, WK, axis=1)
        wpos = q_pos0 + q0 - WINDOW + jnp.arange(WK)
        wmask = ((wpos[None, :] <= qpos[:, None]) & (wpos[None, :] > qpos[:, None] - WINDOW) &
                 (wpos[None, :] >= kw_pos0))
        s_w = jnp.einsum('bqghd,bkgd->bghqk', qb, kbw).astype(F32) * scale
        p_w = _masked_softmax(s_w, wmask)
        o_w = jnp.einsum('bghqk,bkgd->bqghd', p_w.astype(vbw.dtype), vbw)
        out = gb[..., 0:1] * o_c + gb[..., 1:2] * o_s + gb[..., 2:3] * o_w
        return out.astype(q.dtype)

    outs = lax.map(block, jnp.arange(nqb))
    return jnp.moveaxis(outs, 0, 1).reshape(B, Tq, N_WIDTH)


def _mlstm_half(z, c0, n0, m0, prm, B, T):
    o = IN_OFFS
    mq = z[:, o[0]:o[1]].reshape(B, T, M_HEADS, M_DQK)
    mk = z[:, o[1]:o[2]].reshape(B, T, M_HEADS, M_DQK) * (M_DQK ** -0.5)
    mv = z[:, o[2]:o[3]].reshape(B, T, M_HEADS, M_DV)
    mo = z[:, o[3]:o[4]].reshape(B, T, M_WIDTH)
    mi = z[:, Z_G0:Z_G0 + M_HEADS].reshape(B, T, M_HEADS)
    mf = z[:, Z_G0 + M_HEADS:Z_G0 + 2 * M_HEADS].reshape(B, T, M_HEADS)
    ig = _softcap(mi + prm['b_mlstm_i'], GATE_CAP)
    lf = jax.nn.log_sigmoid(_softcap(mf + prm['b_mlstm_f'], GATE_CAP))
    h, c1, n1, m1 = _mlstm_mix(mq, mk, mv, ig, lf, c0, n0, m0)
    h = _rmsnorm(h, prm['mlstm_out_norm']).reshape(B, T, M_WIDTH)
    return h * jax.nn.sigmoid(mo), c1, n1, m1


def _gate_cols(z):
    return z[:, Z_G0 + 2 * M_HEADS:Z_G0 + 2 * M_HEADS + 3 * N_HEADS]


def _mix_fresh(x, prm, tm):
    B, T, _ = x.shape
    rows = B * T
    z = _matmul(x.reshape(rows, D_MODEL), prm['norm_mix'], prm['w_in_bf'], tm=tm, norm=True)
    c0 = jnp.zeros((B, M_HEADS, M_DQK, M_DV), F32)
    n0 = jnp.zeros((B, M_HEADS, M_DQK), F32)
    m0 = jnp.zeros((B, M_HEADS), F32)
    y_m, c1, n1, m1 = _mlstm_half(z, c0, n0, m0, prm, B, T)

    kvc = z[:, Z_KVC0:Z_KVC0 + KV_ROW].reshape(B, T, N_KV, N_KVW)
    kvs_rows, ks, vs = _kv_prep(z, Z_KVS0 // KV_ROW, prm['k_norm_slc'], tm=tm)
    kvw_rows, kw, vw = _kv_prep(z, Z_KVW0 // KV_ROW, prm['k_norm_win'], tm=tm)
    kc = _rmsnorm(_compress_blocks(kvc[..., :N_DQK], prm['cmp_pe_k'], prm['cmp_w1_k'], prm['cmp_w2_k']),
                  prm['k_norm_cmp'])
    vc = _compress_blocks(kvc[..., N_DQK:], prm['cmp_pe_v'], prm['cmp_w1_v'], prm['cmp_w2_v'])
    n_cmp = kc.shape[1]
    nc_pad = _round_up(n_cmp, LANE)
    kc_g = jnp.pad(jnp.moveaxis(kc, 2, 0), ((0, 0), (0, 0), (0, nc_pad - n_cmp), (0, Q_PAD - N_DQK)))
    vc_g = jnp.pad(jnp.moveaxis(vc, 2, 0), ((0, 0), (0, 0), (0, nc_pad - n_cmp), (0, 0)))
    kc_g = kc_g.astype(BF16).reshape(N_KV, B * nc_pad, Q_PAD)
    vc_g = vc_g.astype(BF16).reshape(N_KV, B * nc_pad, N_DV)
    graw = jnp.moveaxis(_gate_cols(z).reshape(rows, N_KV, 3 * N_HPG), 1, 0)
    graw = jnp.pad(graw, ((0, 0), (0, 0), (0, LANE - 3 * N_HPG)))
    y_n = _nsa_fresh(z, graw, ks, vs, kw, vw, kc_g, vc_g, prm['q_norm'], batch=B, t_len=T)

    mixed = jnp.concatenate([y_m.reshape(rows, M_WIDTH), y_n], axis=-1)
    x2 = x.reshape(rows, D_MODEL) + _matmul(mixed, None, prm['w_out_bf'], tm=tm, norm=False)
    keep = min(WINDOW, T)
    kv_win = kvw_rows.reshape(B, T, N_KV, N_KVW)[:, T - keep:]
    return x2, kvc, kvs_rows.reshape(B, T, N_KV, N_KVW), kv_win, c1, n1, m1


def _nsa_paged(q, gates, cache_cmp, cache_slc, win_buf, page_table, new_slc, new_win, prm):
    B, T = q.shape[:2]
    page = cache_cmp.shape[1]
    n_past = page_table.shape[1] * page
    assert T <= 8 and n_past % SEL_LEN == 0 and n_past % CMP_STRIDE == 0 and T < CMP_STRIDE
    assert (N_HPG * T * N_KV) % 8 == 0
    scale = N_DQK ** -0.5
    qpos = n_past + jnp.arange(T)
    qg = q.reshape(B, T, N_KV, N_HPG, N_DQK)
    gg = gates.reshape(B, T, N_KV, N_HPG, 3)

    pk, pv = _cmp_partial(cache_cmp, page_table, prm['cmp_w1_k'], prm['cmp_w1_v'])

    def finish(p, d, pe, w1, w2):
        pre = p[:, :, :-1, :d] + p[:, :, 1:, d:] + pe.reshape(-1) @ w1
        return jax.nn.gelu(pre) @ w2

    kc = _rmsnorm(finish(pk, N_DQK, prm['cmp_pe_k'], prm['cmp_w1_k'], prm['cmp_w2_k']), prm['k_norm_cmp'])
    vc = finish(pv, N_DV, prm['cmp_pe_v'], prm['cmp_w1_v'], prm['cmp_w2_v'])
    n_cmp = kc.shape[2]
    cmp_start = jnp.arange(n_cmp) * CMP_STRIDE
    cmp_end = cmp_start + CMP_LEN - 1
    s_c = jnp.einsum('bqghd,bgnd->bghqn', qg, kc).astype(F32) * scale
    p_c = _masked_softmax(s_c, cmp_end[None, :] <= qpos[:, None])
    o_c = jnp.einsum('bghqn,bgnd->bqghd', p_c, vc)

    n_sel = -(-(n_past + T) // SEL_LEN)
    top = min(SEL_TOPK, n_sel)
    sel_start = jnp.arange(n_sel) * SEL_LEN
    overlap = ((cmp_start[:, None] <= sel_start[None, :] + SEL_LEN - 1) &
               (cmp_end[:, None] >= sel_start[None, :])).astype(F32)
    imp = jnp.einsum('bghqn,nj->bgqj', p_c, overlap)
    jj = jnp.arange(n_sel)
    cur = qpos // SEL_LEN
    valid = sel_start[None, :] <= qpos[:, None]
    forced = (jj[None, :] == 0) | ((jj[None, :] <= cur[:, None]) & (jj[None, :] > cur[:, None] - N_LOCAL))
    rank = jnp.where(forced, jnp.inf, jnp.where(valid, imp, -jnp.inf))
    _, idx = lax.top_k(rank, top)
    chosen = jnp.any(idx[..., None] == jj, axis=-2)
    n_past_blk = n_past // SEL_LEN
    steps = page_table.shape[1] // PAGES_PER_STEP
    blk_step = n_past_blk // steps
    sel = jnp.broadcast_to(chosen[:, :, None, :, :n_past_blk], (B, N_KV, N_HPG, T, n_past_blk))
    sel = sel.reshape(B, N_KV * N_HPG * T, steps, blk_step)
    sel = jnp.pad(jnp.moveaxis(sel, 2, 1), ((0, 0), (0, 0), (0, 0), (0, LANE - blk_step))).astype(BF16)

    qt = jnp.pad(jnp.transpose(qg, (0, 2, 3, 1, 4)), ((0, 0),) * 4 + ((0, N_DV),))
    qx = qt.reshape(B, N_KV * N_HPG * T, N_KVW)
    pad8 = lambda a: jnp.pad(a, ((0, 0), (0, 8 - T), (0, 0), (0, 0)))
    o_s, o_w = _nsa_cached(cache_slc, page_table, qx.astype(BF16), sel, pad8(new_slc), win_buf, pad8(new_win),
                           t_new=T)
    to_q = lambda o: jnp.transpose(o.reshape(B, N_KV, N_HPG, T, N_DV), (0, 3, 1, 2, 4))
    out = gg[..., 0:1] * o_c + gg[..., 1:2] * to_q(o_s) + gg[..., 2:3] * to_q(o_w)
    return out.reshape(B, T, N_WIDTH)


def _mix_cached(x, cache_cmp, cache_slc, win_buf, page_table, c0, n0, m0, prm, tm):
    B, T, _ = x.shape
    rows = B * T
    z = _matmul(x.reshape(rows, D_MODEL), prm['norm_mix'], prm['w_in_bf'], tm=tm, norm=True)
    y_m, c1, n1, m1 = _mlstm_half(z, c0, n0, m0, prm, B, T)
    nq = z[:, Z_Q0:Z_KVC0].reshape(B, T, N_HEADS, Q_PAD)[..., :N_DQK]
    q = _rmsnorm(nq, prm['q_norm'])
    kvc = z[:, Z_KVC0:Z_KVC0 + KV_ROW].reshape(B, T, N_KV, N_KVW)
    kvs_rows, _, _ = _kv_prep(z, Z_KVS0 // KV_ROW, prm['k_norm_slc'], tm=tm)
    kvw_rows, _, _ = _kv_prep(z, Z_KVW0 // KV_ROW, prm['k_norm_win'], tm=tm)
    gates = jax.nn.sigmoid(_gate_cols(z)).reshape(B, T, N_HEADS, 3)
    y_n = _nsa_paged(q, gates, cache_cmp, cache_slc, win_buf, page_table,
                     kvs_rows.reshape(B, T, N_KV, N_KVW), kvw_rows.reshape(B, T, N_KV, N_KVW), prm)
    mixed = jnp.concatenate([y_m, y_n], axis=-1).reshape(rows, D_MODEL)
    x2 = x.reshape(rows, D_MODEL) + _matmul(mixed, None, prm['w_out_bf'], tm=tm, norm=False)
    n_past = page_table.shape[1] * cache_cmp.shape[1]
    keep = min(WINDOW, n_past + T)
    all_w = jnp.concatenate([win_buf, kvw_rows.reshape(B, T, N_KV, N_KVW)], axis=1)
    return (x2, kvc, kvs_rows.reshape(B, T, N_KV, N_KVW), all_w[:, all_w.shape[1] - keep:], c1, n1, m1)


def _ple_half(x3, ple, prm, tm):
    gate = _matmul(x3, prm['norm_ple'], prm['w_ple_gate_bf'], tm=tm, norm=True)
    proj = _matmul(ple, None, prm['w_ple_proj_bf'], tm=tm, norm=False)
    return x3 + jax.nn.sigmoid(gate) * proj


def kernel(x_prompt, x_sample, p_prompt, p_sample, cache_kv_cmp, cache_kv_slc, cache_kv_win, state_mlstm_C, state_mlstm_n, state_mlstm_m, page_table, norm_mix, w_in, b_mlstm_i, b_mlstm_f, mlstm_out_norm, q_norm, k_norm_cmp, k_norm_slc, k_norm_win, cmp_pe_k, cmp_w1_k, cmp_w2_k, cmp_pe_v, cmp_w1_v, cmp_w2_v, w_out, norm_ffn, w_router, b_router, w_gate, b_gate, w_up, b_up, w_down, b_down, norm_ple, w_ple_gate, w_ple_proj):
    depth = w_in.shape[0]
    assert depth == 1
    l = 0
    bp, sp, _ = x_prompt.shape
    db, ds, _ = x_sample.shape
    n_past = page_table.shape[1] * cache_kv_cmp.shape[2]
    prm = dict(norm_mix=norm_mix[l], b_mlstm_i=b_mlstm_i[l], b_mlstm_f=b_mlstm_f[l],
               mlstm_out_norm=mlstm_out_norm[l], q_norm=q_norm[l], k_norm_cmp=k_norm_cmp[l],
               k_norm_slc=k_norm_slc[l], k_norm_win=k_norm_win[l], cmp_pe_k=cmp_pe_k[l],
               cmp_w1_k=cmp_w1_k[l], cmp_w2_k=cmp_w2_k[l], cmp_pe_v=cmp_pe_v[l], cmp_w1_v=cmp_w1_v[l],
               cmp_w2_v=cmp_w2_v[l], norm_ffn=norm_ffn[l], w_router=w_router[l],
               b_router=b_router[l], w_gate=w_gate[l], b_gate=b_gate[l], w_up=w_up[l], b_up=b_up[l],
               w_down=w_down[l], b_down=b_down[l], norm_ple=norm_ple[l])
    prm['w_in_bf'] = _relayout_w_in(w_in[l])
    prm['w_out_bf'] = w_out[l].astype(BF16)
    prm['w_ple_gate_bf'] = w_ple_gate[l].astype(BF16)
    prm['w_ple_proj_bf'] = w_ple_proj[l].astype(BF16)

    tm_p, tm_s = 512, db * ds
    xp2, a1, a2, a3, a4, a5, a6 = _mix_fresh(x_prompt, prm, tm_p)
    xs2, b1, b2, b3, b4, b5, b6 = _mix_cached(x_sample, cache_kv_cmp[l], cache_kv_slc[l], cache_kv_win[l],
                                              page_table, state_mlstm_C[l], state_mlstm_n[l],
                                              state_mlstm_m[l], prm, tm_s)

    fp, lp = _router(xp2, prm['norm_ffn'], prm['w_router'], prm['b_router'], tm=tm_p)
    fs, ls = _router(xs2, prm['norm_ffn'], prm['w_router'], prm['b_router'], tm=tm_s)
    y_moe = _moe_ffn(jnp.concatenate([fp, fs], axis=0), jnp.concatenate([lp, ls], axis=0), prm)
    xp3 = xp2 + y_moe[:bp * sp]
    xs3 = xs2 + y_moe[bp * sp:]

    y_p = _ple_half(xp3, p_prompt[l].reshape(bp * sp, -1), prm, tm_p).reshape(bp, sp, D_MODEL)
    y_s = _ple_half(xs3, p_sample[l].reshape(db * ds, -1), prm, tm_s).reshape(db, ds, D_MODEL)

    st = lambda a: a[None]
    return (y_p, y_s, st(a1), st(a2), st(a3), st(a4), st(a5), st(a6),
            st(b1), st(b2), st(b3), st(b4), st(b5), st(b6))
```

```python
import functools

import numpy as np
import jax
import jax.numpy as jnp
from jax import lax
from jax.experimental import pallas as pl
from jax.experimental.pallas import tpu as pltpu

D_MODEL = 4096
M_HEADS = 4
M_DV = D_MODEL // (2 * M_HEADS)
M_DQK = M_DV // 2
M_WIDTH = M_HEADS * M_DV
M_CHUNK = 64
GATE_CAP = 15.0
N_HEADS = 16
N_KV = 4
N_HPG = N_HEADS // N_KV
N_DV = D_MODEL // (2 * N_HEADS)
N_DQK = 192
N_KVW = N_DQK + N_DV
N_WIDTH = N_HEADS * N_DV
CMP_LEN = 32
CMP_STRIDE = 16
SEL_LEN = 64
SEL_TOPK = 16
N_LOCAL = 2
WINDOW = 512
NSA_QBLK = 64
N_EXPERTS = 32
TOP_K = 4
SWIGLU_LIMIT = 7.0
SWIGLU_ALPHA = 1.702
EPS = 1e-6

IN_SIZES = (M_HEADS * M_DQK, M_HEADS * M_DQK, M_WIDTH, M_WIDTH, M_HEADS, M_HEADS,
            N_HEADS * N_DQK, N_KV * N_KVW, N_KV * N_KVW, N_KV * N_KVW, 3 * N_HEADS)
N_IN = sum(IN_SIZES)
IN_OFFS = tuple(int(v) for v in np.cumsum((0,) + IN_SIZES))

VMEM_LIMIT_BYTES = 56 * 1024 * 1024
LANE = 128
MM_TN = 512
MOE_TM = 256
MOE_TN = 512

BF16 = jnp.bfloat16
F32 = jnp.float32
NEG_INF = float("-inf")


def _round_up(a, b):
    return (a + b - 1) // b * b


Q_PAD = 256
Z_Q0 = IN_OFFS[4]
Z_KVC0 = Z_Q0 + N_HEADS * Q_PAD
Z_KVS0 = Z_KVC0 + N_KV * N_KVW
Z_KVW0 = Z_KVS0 + N_KV * N_KVW
Z_G0 = Z_KVW0 + N_KV * N_KVW
N_Z = _round_up(Z_G0 + LANE, MM_TN)
KV_ROW = N_KV * N_KVW
KV_PERM = N_KV * Q_PAD + N_KV * N_DV


def _dot(a, b):
    return jnp.dot(a, b, preferred_element_type=F32)


def _dot_nt(a, b):
    return lax.dot_general(a, b, (((1,), (1,)), ((), ())), preferred_element_type=F32)


def _split_bf16(a):
    hi = a.astype(BF16)
    return hi, (a - hi.astype(F32)).astype(BF16)


def _mm_kernel(a_ref, g_ref, w_ref, o_ref, a_bf, *, norm):
    @pl.when(pl.program_id(1) == 0)
    def _():
        a = a_ref[...]
        if norm:
            ms = jnp.mean(a * a, axis=-1, keepdims=True)
            a = a * lax.rsqrt(ms + EPS) * g_ref[...]
        a_bf[...] = a.astype(BF16)

    o_ref[...] = _dot(a_bf[...], w_ref[...])


def _matmul(a, gain, w_bf, *, tm, tn=MM_TN, norm):
    m, k = a.shape
    n = w_bf.shape[1]
    assert m % tm == 0 and n % tn == 0 and w_bf.shape[0] == k
    g = (jnp.ones((k,), F32) if gain is None else gain.astype(F32)).reshape(1, k)
    return pl.pallas_call(
        functools.partial(_mm_kernel, norm=norm),
        grid=(m // tm, n // tn),
        in_specs=[pl.BlockSpec((tm, k), lambda i, j: (i, 0)),
                  pl.BlockSpec((1, k), lambda i, j: (0, 0)),
                  pl.BlockSpec((k, tn), lambda i, j: (0, j))],
        out_specs=pl.BlockSpec((tm, tn), lambda i, j: (i, j)),
        out_shape=jax.ShapeDtypeStruct((m, n), F32),
        scratch_shapes=[pltpu.VMEM((tm, k), BF16)],
        compiler_params=pltpu.CompilerParams(
            dimension_semantics=("parallel", "arbitrary"),
            vmem_limit_bytes=VMEM_LIMIT_BYTES),
        name="dense_proj",
    )(a, g, w_bf)


def _relayout_w_in(w):
    d = w.shape[0]
    o = IN_OFFS
    wb = w.astype(BF16)
    cols = [wb[:, :o[4]]]
    zq = jnp.zeros((d, Q_PAD - N_DQK), BF16)
    for h in range(N_HEADS):
        cols += [wb[:, o[6] + h * N_DQK:o[6] + (h + 1) * N_DQK], zq]
    cols += [wb[:, o[7]:o[10]], wb[:, o[4]:o[6]], wb[:, o[10]:o[11]]]
    n_used = Z_G0 + (o[6] - o[4]) + (o[11] - o[10])
    cols += [jnp.zeros((d, N_Z - n_used), BF16)]
    return jnp.concatenate(cols, axis=1)


def _kv_prep_kernel(x_ref, gain_ref, vmask_ref, ind_ref, indt_ref, perm_ref, rows_ref, k_ref, v_ref):
    x = x_ref[...]
    sq_hi, sq_lo = _split_bf16(x * x)
    ss = _dot(sq_hi, ind_ref[...]) + _dot(sq_lo, ind_ref[...])
    inv_hi, inv_lo = _split_bf16(lax.rsqrt(ss * (1.0 / N_DQK) + EPS))
    scale = _dot(inv_hi, indt_ref[...]) + _dot(inv_lo, indt_ref[...]) + vmask_ref[...]
    y = x * scale * gain_ref[...]
    rows_ref[...] = y
    kv = _dot(y.astype(BF16), perm_ref[...])
    for g in range(N_KV):
        k_ref[g] = kv[:, g * Q_PAD:(g + 1) * Q_PAD].astype(BF16)
        v_ref[g] = kv[:, N_KV * Q_PAD + g * N_DV:N_KV * Q_PAD + (g + 1) * N_DV].astype(BF16)


def _kv_layout_constants():
    ind = np.zeros((KV_ROW, LANE), np.float32)
    vmask = np.zeros((1, KV_ROW), np.float32)
    perm = np.zeros((KV_ROW, KV_PERM), np.float32)
    for g in range(N_KV):
        ind[g * N_KVW:g * N_KVW + N_DQK, g] = 1.0
        vmask[0, g * N_KVW + N_DQK:(g + 1) * N_KVW] = 1.0
        for c in range(N_DQK):
            perm[g * N_KVW + c, g * Q_PAD + c] = 1.0
        for c in range(N_DV):
            perm[g * N_KVW + N_DQK + c, N_KV * Q_PAD + g * N_DV + c] = 1.0
    return ind, vmask, perm


def _kv_prep(z, col_block, k_gain, *, tm):
    rows = z.shape[0]
    ind, vmask, perm = _kv_layout_constants()
    gain_row = jnp.concatenate([k_gain.astype(F32), jnp.ones((N_DV,), F32)])
    gain_map = jnp.tile(gain_row, N_KV).reshape(1, KV_ROW)
    const = lambda shape: pl.BlockSpec(shape, lambda i: (0, 0))
    return pl.pallas_call(
        _kv_prep_kernel,
        grid=(rows // tm,),
        in_specs=[pl.BlockSpec((tm, KV_ROW), lambda i: (i, col_block)),
                  const((1, KV_ROW)), const((1, KV_ROW)), const((KV_ROW, LANE)),
                  const((LANE, KV_ROW)), const((KV_ROW, KV_PERM))],
        out_specs=[pl.BlockSpec((tm, KV_ROW), lambda i: (i, 0)),
                   pl.BlockSpec((N_KV, tm, Q_PAD), lambda i: (0, i, 0)),
                   pl.BlockSpec((N_KV, tm, N_DV), lambda i: (0, i, 0))],
        out_shape=[jax.ShapeDtypeStruct((rows, KV_ROW), F32),
                   jax.ShapeDtypeStruct((N_KV, rows, Q_PAD), BF16),
                   jax.ShapeDtypeStruct((N_KV, rows, N_DV), BF16)],
        compiler_params=pltpu.CompilerParams(
            dimension_semantics=("parallel",),
            vmem_limit_bytes=VMEM_LIMIT_BYTES),
        name="kv_prep",
    )(z, gain_map, jnp.asarray(vmask), jnp.asarray(ind, BF16), jnp.asarray(ind.T.copy(), BF16),
      jnp.asarray(perm, BF16))


def _exp_parts(s):
    mx = jnp.max(s, axis=-1, keepdims=True)
    mx = jnp.where(jnp.abs(mx) < jnp.inf, mx, 0.0)
    e = jnp.exp(s - mx)
    den = jnp.sum(e, axis=-1, keepdims=True)
    return e, jnp.where(den > 0, den, 1.0)


def _nsa_fresh_kernel(zq_ref, gt_ref, ks_ref, vs_ref, kw_ref, vw_ref, kc_ref, vc_ref,
                      qg_ref, ov_ref, ex_ref, o_ref, *, t_len, top, n_win):
    qb = NSA_QBLK
    q0 = pl.program_id(2) * qb
    scale = N_DQK ** -0.5
    n_sel = t_len // SEL_LEN

    parts = []
    for h in range(N_HPG):
        qh = zq_ref[:, h * Q_PAD:(h + 1) * Q_PAD]
        ms = jnp.sum(qh * qh, axis=-1, keepdims=True) * (1.0 / N_DQK)
        parts.append((qh * lax.rsqrt(ms + EPS) * qg_ref[...]).astype(BF16))
    qn = jnp.concatenate(parts, axis=0)
    qpos1 = q0 + lax.broadcasted_iota(jnp.int32, (qb, 1), 0)
    qpos = jnp.concatenate([qpos1] * N_HPG, axis=0)

    nc_pad = kc_ref.shape[1]
    cmp_end = lax.broadcasted_iota(jnp.int32, (1, nc_pad), 1) * CMP_STRIDE + (CMP_LEN - 1)
    s_c = jnp.where(cmp_end <= qpos, _dot_nt(qn, kc_ref[0]) * scale, NEG_INF)
    e_c, den_c = _exp_parts(s_c)
    p_c = e_c / den_c
    o_c = _dot(p_c.astype(BF16), vc_ref[0])

    psum = p_c[0:qb]
    for h in range(1, N_HPG):
        psum = psum + p_c[h * qb:(h + 1) * qb]
    ps_hi, ps_lo = _split_bf16(psum)
    imp = _dot(ps_hi, ov_ref[...]) + _dot(ps_lo, ov_ref[...])
    jl = lax.broadcasted_iota(jnp.int32, (1, LANE), 1)
    cur = qpos1 // SEL_LEN
    valid = jl * SEL_LEN <= qpos1
    forced = jnp.logical_or(jl == 0, jnp.logical_and(jl <= cur, jl > cur - N_LOCAL))
    rank = jnp.where(forced, jnp.inf, jnp.where(valid, imp, NEG_INF))
    ahead = jnp.zeros((qb, LANE), F32)
    for i in range(n_sel):
        ri = rank[:, i:i + 1]
        before = jnp.logical_or(ri > rank, jnp.logical_and(ri == rank, jl > i))
        ahead = ahead + jnp.where(before, 1.0, 0.0)
    sel = jnp.where(jnp.logical_and(ahead < top, valid), 1.0, 0.0).astype(BF16)
    selx = _dot(sel, ex_ref[...])
    kpos = lax.broadcasted_iota(jnp.int32, (1, t_len), 1)
    bias1 = jnp.where(jnp.logical_and(selx > 0.5, kpos <= qpos1), 0.0, NEG_INF)
    bias_s = jnp.concatenate([bias1] * N_HPG, axis=0)

    e_s, den_s = _exp_parts(_dot_nt(qn, ks_ref[0]) * scale + bias_s)
    o_s = _dot(e_s.astype(BF16), vs_ref[0]) / den_s

    ws = pl.multiple_of(jnp.maximum(q0 - WINDOW, 0), qb)
    wpos = ws + lax.broadcasted_iota(jnp.int32, (1, n_win), 1)
    mask_w = jnp.logical_and(wpos <= qpos, wpos > qpos - WINDOW)
    s_w = jnp.where(mask_w, _dot_nt(qn, kw_ref[0, pl.ds(ws, n_win), :]) * scale, NEG_INF)
    e_w, den_w = _exp_parts(s_w)
    o_w = _dot(e_w.astype(BF16), vw_ref[0, pl.ds(ws, n_win), :]) / den_w

    gt = jax.nn.sigmoid(gt_ref[0])
    for h in range(N_HPG):
        r = slice(h * qb, (h + 1) * qb)
        o_ref[:, h * N_DV:(h + 1) * N_DV] = (gt[:, 3 * h:3 * h + 1] * o_c[r]
                                             + gt[:, 3 * h + 1:3 * h + 2] * o_s[r]
                                             + gt[:, 3 * h + 2:3 * h + 3] * o_w[r])


def _nsa_fresh(z, graw, ks, vs, kw, vw, kc, vc, q_gain, *, batch, t_len, top=SEL_TOPK):
    qb = NSA_QBLK
    assert t_len % qb == 0 and t_len % SEL_LEN == 0 and SEL_LEN == qb
    nqb = t_len // qb
    n_sel = t_len // SEL_LEN
    n_cmp = t_len // CMP_STRIDE - CMP_LEN // CMP_STRIDE + 1
    nc_pad = kc.shape[1] // batch
    n_win = min(WINDOW + qb, t_len)
    cmp_start = np.arange(nc_pad) * CMP_STRIDE
    sel_start = np.arange(LANE) * SEL_LEN
    ov = ((cmp_start[:, None] <= sel_start[None, :] + SEL_LEN - 1) &
          (cmp_start[:, None] + CMP_LEN - 1 >= sel_start[None, :]) &
          (np.arange(nc_pad)[:, None] < n_cmp) & (np.arange(LANE)[None, :] < n_sel))
    ex = (np.arange(t_len)[None, :] // SEL_LEN) == np.arange(LANE)[:, None]
    qg = jnp.concatenate([q_gain.astype(F32), jnp.zeros((Q_PAD - N_DQK,), F32)]).reshape(1, Q_PAD)
    qw = N_HPG * Q_PAD
    return pl.pallas_call(
        functools.partial(_nsa_fresh_kernel, t_len=t_len, top=min(top, n_sel), n_win=n_win),
        grid=(batch, N_KV, nqb),
        in_specs=[pl.BlockSpec((qb, qw), lambda b, g, j: (b * nqb + j, Z_Q0 // qw + g)),
                  pl.BlockSpec((1, qb, LANE), lambda b, g, j: (g, b * nqb + j, 0)),
                  pl.BlockSpec((1, t_len, Q_PAD), lambda b, g, j: (g, b, 0)),
                  pl.BlockSpec((1, t_len, N_DV), lambda b, g, j: (g, b, 0)),
                  pl.BlockSpec((1, t_len, Q_PAD), lambda b, g, j: (g, b, 0)),
                  pl.BlockSpec((1, t_len, N_DV), lambda b, g, j: (g, b, 0)),
                  pl.BlockSpec((1, nc_pad, Q_PAD), lambda b, g, j: (g, b, 0)),
                  pl.BlockSpec((1, nc_pad, N_DV), lambda b, g, j: (g, b, 0)),
                  pl.BlockSpec((1, Q_PAD), lambda b, g, j: (0, 0)),
                  pl.BlockSpec((nc_pad, LANE), lambda b, g, j: (0, 0)),
                  pl.BlockSpec((LANE, t_len), lambda b, g, j: (0, 0))],
        out_specs=pl.BlockSpec((qb, N_HPG * N_DV), lambda b, g, j: (b * nqb + j, g)),
        out_shape=jax.ShapeDtypeStruct((batch * t_len, N_WIDTH), F32),
        compiler_params=pltpu.CompilerParams(
            dimension_semantics=("parallel", "parallel", "arbitrary"),
            vmem_limit_bytes=VMEM_LIMIT_BYTES),
        name="nsa_fresh",
    )(z, graw, ks, vs, kw, vw, kc, vc, qg, jnp.asarray(ov, BF16), jnp.asarray(ex, BF16))


PAGES_PER_STEP = 8
ROW_TILES = -(-N_KVW // LANE)


def _keys_minor(rows):
    return jnp.transpose(rows, (0, 2, 3, 1))


def _page_specs(page, n_pages):
    def spec(i):
        return pl.BlockSpec((1, N_KV, N_KVW, page),
                            lambda b, c, pt: (pt[b * n_pages + c * PAGES_PER_STEP + i], 0, 0, 0))
    return [spec(i) for i in range(PAGES_PER_STEP)]


def _cmp_partial_kernel(pt_ref, *refs, page):
    pages = refs[:PAGES_PER_STEP]
    wk0_ref, wk1_ref, wv1_ref, wv2_ref, pk_ref, pv_ref, stage = refs[PAGES_PER_STEP:]
    keys = PAGES_PER_STEP * page
    n_sub = keys // CMP_STRIDE
    tail = jnp.zeros((ROW_TILES * LANE - N_KVW, page), F32)
    for g in range(N_KV):
        for i, p in enumerate(pages):
            xt = p[0, g]
            r0 = g * keys + i * page
            stage[0, r0:r0 + page, :] = xt[:LANE].T
            stage[1, r0:r0 + page, :] = xt[LANE:2 * LANE].T
            stage[2, r0:r0 + page, :] = jnp.concatenate([xt[2 * LANE:], tail], axis=0).T
    acc_k = jnp.zeros((N_KV * n_sub, 2 * N_DQK), F32)
    acc_v = jnp.zeros((N_KV * n_sub, 2 * N_DV), F32)
    for t in range(CMP_STRIDE):
        a0, a1, a2 = [stage[j, pl.ds(t, N_KV * n_sub, stride=CMP_STRIDE), :].astype(BF16)
                      for j in range(ROW_TILES)]
        acc_k = acc_k + _dot(a0, wk0_ref[t]) + _dot(a1, wk1_ref[t])
        acc_v = acc_v + _dot(a1, wv1_ref[t]) + _dot(a2, wv2_ref[t])
    for g in range(N_KV):
        pk_ref[0, g] = acc_k[g * n_sub:(g + 1) * n_sub]
        pv_ref[0, g] = acc_v[g * n_sub:(g + 1) * n_sub]


def _cmp_partial(cache, page_table, w1_k, w1_v):
    n_pool, page = cache.shape[:2]
    batch, n_pages = page_table.shape
    assert page % CMP_STRIDE == 0 and n_pages % PAGES_PER_STEP == 0 and CMP_LEN == 2 * CMP_STRIDE
    assert ROW_TILES == 3 and LANE < N_DQK <= 2 * LANE < N_KVW
    sub_step = PAGES_PER_STEP * page // CMP_STRIDE
    n_sub = n_pages * page // CMP_STRIDE

    def halves(w1, d):
        w = w1.astype(BF16).reshape(2, CMP_STRIDE, d, d)
        return jnp.concatenate([w[0], w[1]], axis=-1)

    wk, wv = halves(w1_k, N_DQK), halves(w1_v, N_DV)
    zrow = lambda n, w: jnp.zeros((CMP_STRIDE, n, w.shape[-1]), BF16)
    wk0 = wk[:, :LANE]
    wk1 = jnp.concatenate([wk[:, LANE:], zrow(2 * LANE - N_DQK, wk)], axis=1)
    wv1 = jnp.concatenate([zrow(N_DQK - LANE, wv), wv[:, :2 * LANE - N_DQK]], axis=1)
    wv2 = jnp.concatenate([wv[:, 2 * LANE - N_DQK:], zrow(ROW_TILES * LANE - N_KVW, wv)], axis=1)

    const = lambda w: pl.BlockSpec(w.shape, lambda b, c, pt: (0, 0, 0))
    out = lambda d: pl.BlockSpec((1, N_KV, sub_step, 2 * d), lambda b, c, pt: (b, 0, c, 0))
    return pl.pallas_call(
        functools.partial(_cmp_partial_kernel, page=page),
        grid_spec=pltpu.PrefetchScalarGridSpec(
            num_scalar_prefetch=1,
            grid=(batch, n_pages // PAGES_PER_STEP),
            in_specs=_page_specs(page, n_pages) + [const(wk0), const(wk1), const(wv1), const(wv2)],
            out_specs=[out(N_DQK), out(N_DV)],
            scratch_shapes=[pltpu.VMEM((ROW_TILES, N_KV * PAGES_PER_STEP * page, LANE), F32)]),
        out_shape=[jax.ShapeDtypeStruct((batch, N_KV, n_sub, 2 * N_DQK), F32),
                   jax.ShapeDtypeStruct((batch, N_KV, n_sub, 2 * N_DV), F32)],
        compiler_params=pltpu.CompilerParams(
            dimension_semantics=("parallel", "arbitrary"),
            vmem_limit_bytes=VMEM_LIMIT_BYTES),
        name="cmp_partial",
    )(page_table.reshape(-1), *([_keys_minor(cache)] * PAGES_PER_STEP), wk0, wk1, wv1, wv2)


def _group_rows(refs, g):
    kt = jnp.concatenate([r[0, g, :N_DQK, :] for r in refs], axis=-1).astype(BF16)
    vt = jnp.concatenate([r[0, g, N_DQK:, :] for r in refs], axis=-1).astype(BF16)
    return kt, vt


def _group_scores(qx, krows):
    rg = qx.shape[0] // N_KV
    s = [_dot(qx[g * rg:(g + 1) * rg], krows[g][0]) for g in range(N_KV)]
    return jnp.concatenate(s, axis=0) * (N_DQK ** -0.5)


def _group_pv(p, krows):
    rg = p.shape[0] // N_KV
    return jnp.concatenate([_dot_nt(p[g * rg:(g + 1) * rg], krows[g][1]) for g in range(N_KV)], axis=0)


def _online_update(s, krows, m_sc, l_sc, acc_sc):
    m_old = m_sc[:, :1]
    m_new = jnp.maximum(m_old, jnp.max(s, axis=-1, keepdims=True))
    m_safe = jnp.where(jnp.abs(m_new) < jnp.inf, m_new, 0.0)
    alpha = jnp.exp(m_old - m_safe)
    p = jnp.exp(s - m_safe)
    l_sc[...] = jnp.broadcast_to(alpha * l_sc[:, :1] + jnp.sum(p, axis=-1, keepdims=True), l_sc.shape)
    acc_sc[...] = alpha * acc_sc[...] + _group_pv(p.astype(BF16), krows)
    m_sc[...] = jnp.broadcast_to(m_new, m_sc.shape)


def _nsa_cached_kernel(pt_ref, q_ref, sel_ref, ex_ref, news_ref, win_ref, neww_ref, *refs, t_new):
    pages = refs[:PAGES_PER_STEP]
    os_ref, ow_ref, m_sc, l_sc, acc_sc = refs[PAGES_PER_STEP:]
    c = pl.program_id(1)
    qx = q_ref[0]
    n_q = qx.shape[0]

    @pl.when(c == 0)
    def _():
        m_sc[...] = jnp.full(m_sc.shape, NEG_INF, F32)
        l_sc[...] = jnp.zeros(l_sc.shape, F32)
        acc_sc[...] = jnp.zeros(acc_sc.shape, F32)

    krows = [_group_rows(pages, g) for g in range(N_KV)]
    selx = _dot(sel_ref[0, 0], ex_ref[...])
    s = jnp.where(selx > 0.5, _group_scores(qx, krows), NEG_INF)
    _online_update(s, krows, m_sc, l_sc, acc_sc)

    @pl.when(c == pl.num_programs(1) - 1)
    def _():
        qi = lax.broadcasted_iota(jnp.int32, (n_q, 1), 0) % t_new
        nrows = [_group_rows([news_ref], g) for g in range(N_KV)]
        ki = lax.broadcasted_iota(jnp.int32, (1, news_ref.shape[3]), 1)
        s_n = jnp.where(ki <= qi, _group_scores(qx, nrows), NEG_INF)
        _online_update(s_n, nrows, m_sc, l_sc, acc_sc)
        os_ref[0] = acc_sc[...] / l_sc[:, :1]

        n_buf = win_ref.shape[3]
        wrows = [_group_rows([win_ref, neww_ref], g) for g in range(N_KV)]
        ri = lax.broadcasted_iota(jnp.int32, (1, n_buf + neww_ref.shape[3]), 1)
        vis = jnp.logical_or(jnp.logical_and(ri < n_buf, ri > qi + (n_buf - WINDOW)),
                             jnp.logical_and(ri >= n_buf, ri - n_buf <= qi))
        e_w, den_w = _exp_parts(jnp.where(vis, _group_scores(qx, wrows), NEG_INF))
        ow_ref[0] = _group_pv(e_w.astype(BF16), wrows) / den_w


def _nsa_cached(cache, page_table, qx, sel, new_slc, win_buf, new_win, *, t_new):
    page = cache.shape[1]
    batch, n_pages = page_table.shape
    n_q = qx.shape[1]
    keys_step = PAGES_PER_STEP * page
    assert keys_step % SEL_LEN == 0 and keys_step // SEL_LEN <= LANE and n_pages % PAGES_PER_STEP == 0
    assert win_buf.shape[1] >= WINDOW and t_new <= new_slc.shape[1]
    cache, new_slc, win_buf, new_win = map(_keys_minor, (cache, new_slc, win_buf, new_win))
    ex = (np.arange(keys_step)[None, :] // SEL_LEN) == np.arange(LANE)[:, None]
    per_b = lambda a: pl.BlockSpec((1,) + a.shape[1:], lambda b, c, pt: (b,) + (0,) * (a.ndim - 1))
    out = pl.BlockSpec((1, n_q, N_DV), lambda b, c, pt: (b, 0, 0))
    return pl.pallas_call(
        functools.partial(_nsa_cached_kernel, t_new=t_new),
        grid_spec=pltpu.PrefetchScalarGridSpec(
            num_scalar_prefetch=1,
            grid=(batch, n_pages // PAGES_PER_STEP),
            in_specs=[per_b(qx),
                      pl.BlockSpec((1, 1, n_q, LANE), lambda b, c, pt: (b, c, 0, 0)),
                      pl.BlockSpec((LANE, keys_step), lambda b, c, pt: (0, 0)),
                      per_b(new_slc), per_b(win_buf), per_b(new_win)]
            + _page_specs(page, n_pages),
            out_specs=[out, out],
            scratch_shapes=[pltpu.VMEM((n_q, LANE), F32), pltpu.VMEM((n_q, LANE), F32),
                            pltpu.VMEM((n_q, N_DV), F32)]),
        out_shape=[jax.ShapeDtypeStruct((batch, n_q, N_DV), F32)] * 2,
        compiler_params=pltpu.CompilerParams(
            dimension_semantics=("parallel", "arbitrary"),
            vmem_limit_bytes=VMEM_LIMIT_BYTES),
        name="nsa_cached",
    )(page_table.reshape(-1), qx, sel, jnp.asarray(ex, BF16), new_slc, win_buf, new_win,
      *([cache] * PAGES_PER_STEP))


def _router_kernel(x_ref, g_ref, whi_ref, wlo_ref, b_ref, f_ref, l_ref):
    x = x_ref[...]
    ms = jnp.mean(x * x, axis=-1, keepdims=True)
    f = x * lax.rsqrt(ms + EPS) * g_ref[...]
    f_hi, f_lo = _split_bf16(f)
    f_ref[...] = f_hi
    acc = _dot(f_hi, whi_ref[...]) + _dot(f_lo, whi_ref[...]) + _dot(f_hi, wlo_ref[...])
    l_ref[...] = acc + b_ref[...]


def _router(x, gain, w_router, b_router, *, tm):
    m, k = x.shape
    e = w_router.shape[1]
    w_hi = w_router.astype(BF16)
    w_lo = (w_router - w_hi.astype(F32)).astype(BF16)
    return pl.pallas_call(
        _router_kernel,
        grid=(m // tm,),
        in_specs=[pl.BlockSpec((tm, k), lambda i: (i, 0)),
                  pl.BlockSpec((1, k), lambda i: (0, 0)),
                  pl.BlockSpec((k, e), lambda i: (0, 0)),
                  pl.BlockSpec((k, e), lambda i: (0, 0)),
                  pl.BlockSpec((1, e), lambda i: (0, 0))],
        out_specs=[pl.BlockSpec((tm, k), lambda i: (i, 0)),
                   pl.BlockSpec((tm, e), lambda i: (i, 0))],
        out_shape=[jax.ShapeDtypeStruct((m, k), BF16),
                   jax.ShapeDtypeStruct((m, e), F32)],
        compiler_params=pltpu.CompilerParams(
            dimension_semantics=("parallel",),
            vmem_limit_bytes=VMEM_LIMIT_BYTES),
        name="moe_router",
    )(x, gain.reshape(1, k), w_hi, w_lo, b_router.reshape(1, e))


def _item_flags(blk_ref, exp_ref, st_ref, en_ref, tm):
    w = pl.program_id(1)
    e = exp_ref[w]
    b = blk_ref[w]
    prev = jnp.maximum(w - 1, 0)
    new_expert = jnp.logical_or(w == 0, exp_ref[prev] != e)
    new_block = jnp.logical_or(w == 0, blk_ref[prev] != b)
    rows = b * tm + lax.broadcasted_iota(jnp.int32, (tm, 1), 0)
    mask = jnp.logical_and(rows >= st_ref[e], rows < en_ref[e])
    return w, new_expert, new_block, mask


def _moe_up_kernel(blk_ref, exp_ref, st_ref, en_ref, tot_ref,
                   x_ref, wg_ref, wu_ref, bg_ref, bu_ref, h_ref, wg_bf, wu_bf, *, tm):
    w, new_expert, new_block, mask = _item_flags(blk_ref, exp_ref, st_ref, en_ref, tm)

    @pl.when(w < tot_ref[0])
    def _():
        @pl.when(new_expert)
        def _():
            wg_bf[...] = wg_ref[0].astype(BF16)
            wu_bf[...] = wu_ref[0].astype(BF16)

        x = x_ref[...]
        g = _dot(x, wg_bf[...]) + bg_ref[0]
        u = _dot(x, wu_bf[...]) + bu_ref[0]
        g = jnp.minimum(g, SWIGLU_LIMIT)
        u = jnp.clip(u, -SWIGLU_LIMIT, SWIGLU_LIMIT)
        hdn = (g * jax.nn.sigmoid(SWIGLU_ALPHA * g) * (u + 1.0)).astype(h_ref.dtype)
        keep = jnp.where(new_block, jnp.zeros_like(hdn), h_ref[...])
        h_ref[...] = jnp.where(mask, hdn, keep)


def _moe_down_kernel(blk_ref, exp_ref, st_ref, en_ref, tot_ref,
                     h_ref, wd_ref, bd_ref, y_ref, wd_bf, *, tm):
    w, new_expert, new_block, mask = _item_flags(blk_ref, exp_ref, st_ref, en_ref, tm)

    @pl.when(w < tot_ref[0])
    def _():
        @pl.when(new_expert)
        def _():
            wd_bf[...] = wd_ref[0].astype(BF16)

        y = _dot(h_ref[...], wd_bf[...]) + bd_ref[0]
        keep = jnp.where(new_block, jnp.zeros_like(y), y_ref[...])
        y_ref[...] = jnp.where(mask, y, keep)


def _moe_experts(x_sorted, sched, w_gate, b_gate, w_up, b_up, w_down, b_down):
    blk, exp, starts, ends, total = sched
    rows, d = x_sorted.shape
    n_e, _, d_h = w_gate.shape
    tm, tn = MOE_TM, MOE_TN
    n_items = blk.shape[0]
    params = pltpu.CompilerParams(dimension_semantics=("arbitrary", "arbitrary"),
                                  vmem_limit_bytes=VMEM_LIMIT_BYTES)

    row_spec = pl.BlockSpec((tm, d), lambda j, w, blk, exp, st, en, tot: (blk[w], 0))
    w_spec = pl.BlockSpec((1, d, tn), lambda j, w, blk, exp, st, en, tot: (exp[w], 0, j))
    b_spec = pl.BlockSpec((1, 1, tn), lambda j, w, blk, exp, st, en, tot: (exp[w], 0, j))
    out_spec = pl.BlockSpec((tm, tn), lambda j, w, blk, exp, st, en, tot: (blk[w], j))

    hidden = pl.pallas_call(
        functools.partial(_moe_up_kernel, tm=tm),
        grid_spec=pltpu.PrefetchScalarGridSpec(
            num_scalar_prefetch=5,
            grid=(d_h // tn, n_items),
            in_specs=[row_spec, w_spec, w_spec, b_spec, b_spec],
            out_specs=out_spec,
            scratch_shapes=[pltpu.VMEM((d, tn), BF16), pltpu.VMEM((d, tn), BF16)]),
        out_shape=jax.ShapeDtypeStruct((rows, d_h), BF16),
        compiler_params=params,
        name="moe_up",
    )(blk, exp, starts, ends, total, x_sorted, w_gate, w_up,
      b_gate.reshape(n_e, 1, d_h), b_up.reshape(n_e, 1, d_h))

    return pl.pallas_call(
        functools.partial(_moe_down_kernel, tm=tm),
        grid_spec=pltpu.PrefetchScalarGridSpec(
            num_scalar_prefetch=5,
            grid=(d // tn, n_items),
            in_specs=[row_spec, w_spec, b_spec],
            out_specs=out_spec,
            scratch_shapes=[pltpu.VMEM((d_h, tn), BF16)]),
        out_shape=jax.ShapeDtypeStruct((rows, d), F32),
        compiler_params=params,
        name="moe_down",
    )(blk, exp, starts, ends, total, hidden, w_down, b_down.reshape(n_e, 1, d))


def _moe_schedule(top_e, tm):
    nk = top_e.size
    assert nk % tm == 0
    n_blocks = nk // tm
    n_items = n_blocks + N_EXPERTS - 1
    flat_e = top_e.reshape(nk)
    order = jnp.argsort(flat_e)
    counts = jnp.bincount(flat_e, length=N_EXPERTS).astype(jnp.int32)
    ends = jnp.cumsum(counts).astype(jnp.int32)
    starts = ends - counts
    first_blk = starts // tm
    last_blk = jnp.where(counts > 0, (ends - 1) // tm, first_blk - 1)
    n_it = last_blk - first_blk + 1
    it_end = jnp.cumsum(n_it).astype(jnp.int32)
    it_start = it_end - n_it
    total = it_end[-1]
    w = jnp.arange(n_items, dtype=jnp.int32)
    w_c = jnp.minimum(w, total - 1)
    e_w = jnp.minimum(jnp.searchsorted(it_end, w_c, side='right'), N_EXPERTS - 1).astype(jnp.int32)
    blk_w = (first_blk[e_w] + (w_c - it_start[e_w])).astype(jnp.int32)
    slot_of = jnp.zeros((nk,), jnp.int32).at[order].set(jnp.arange(nk, dtype=jnp.int32))
    return order, slot_of, (blk_w, e_w, starts, ends, total.reshape(1))


def _moe_ffn(f_bf, logits, prm):
    n, d = f_bf.shape
    top_logit, top_e = lax.top_k(logits, TOP_K)
    gate = jax.nn.softmax(top_logit, axis=-1)
    order, slot_of, sched = _moe_schedule(top_e, MOE_TM)
    x_sorted = f_bf[order // TOP_K]
    y_sorted = _moe_experts(x_sorted, sched, prm['w_gate'], prm['b_gate'], prm['w_up'], prm['b_up'],
                            prm['w_down'], prm['b_down'])
    return jnp.einsum('nkd,nk->nd', y_sorted[slot_of].reshape(n, TOP_K, d), gate)


def _rmsnorm(x, g):
    xf = x.astype(F32)
    y = xf * lax.rsqrt(jnp.mean(xf * xf, axis=-1, keepdims=True) + EPS)
    return (y * g.astype(F32)).astype(x.dtype)


def _softcap(x, cap):
    return cap * jnp.tanh(x / cap)


def _masked_softmax(s, mask):
    s = jnp.where(mask, s.astype(F32), -jnp.inf)
    mx = jnp.max(s, axis=-1, keepdims=True)
    e = jnp.exp(s - jnp.where(jnp.isfinite(mx), mx, 0.0))
    den = jnp.sum(e, axis=-1, keepdims=True)
    return e / jnp.where(den > 0, den, 1.0)


def _mlstm_chunk(carry, inp):
    c, n, m = carry
    q, k, v, ig, lf = inp
    L = q.shape[2]
    b = jnp.cumsum(lf, axis=-1)
    causal = jnp.tril(jnp.ones((L, L), bool))
    dmat = jnp.where(causal, b[..., :, None] - b[..., None, :] + ig[..., None, :], -jnp.inf)
    inter = b + m[..., None]
    m_t = jnp.maximum(inter, jnp.max(dmat, axis=-1))
    w_intra = jnp.exp(dmat - m_t[..., None])
    w_inter = jnp.exp(inter - m_t)
    s = jnp.einsum('bhtd,bhsd->bhts', q, k) * w_intra
    num = jnp.einsum('bhts,bhsv->bhtv', s, v) + w_inter[..., None] * jnp.einsum('bhtd,bhdv->bhtv', q, c)
    den = jnp.sum(s, axis=-1) + w_inter * jnp.einsum('bhtd,bhd->bht', q, n)
    h = num / jnp.maximum(jnp.abs(den), jnp.exp(-m_t))[..., None]
    b_last = b[..., -1]
    g = b_last[..., None] - b + ig
    m_new = jnp.maximum(b_last + m, jnp.max(g, axis=-1))
    wk = jnp.exp(g - m_new[..., None])
    decay = jnp.exp(b_last + m - m_new)
    c_new = decay[..., None, None] * c + jnp.einsum('bhs,bhsd,bhsv->bhdv', wk, k, v)
    n_new = decay[..., None] * n + jnp.einsum('bhs,bhsd->bhd', wk, k)
    return (c_new, n_new, m_new), h


def _to_chunks(a, nc, L):
    a = a.astype(F32).reshape((a.shape[0], nc, L) + a.shape[2:])
    return jnp.moveaxis(jnp.moveaxis(a, 3, 2), 1, 0)


def _mlstm_mix(q, k, v, ig, lf, c0, n0, m0):
    B, T, H, _ = q.shape
    L = M_CHUNK if T % M_CHUNK == 0 else T
    nc = T // L
    xs = (_to_chunks(q, nc, L), _to_chunks(k, nc, L), _to_chunks(v, nc, L),
          _to_chunks(ig, nc, L), _to_chunks(lf, nc, L))
    init = (c0.astype(F32), n0.astype(F32), m0.astype(F32))
    (c1, n1, m1), hs = lax.scan(_mlstm_chunk, init, xs)
    h = jnp.moveaxis(jnp.moveaxis(hs, 0, 1), 2, 3).reshape(B, T, H, -1)
    return h, c1, n1, m1


def _compress_blocks(x_raw, pe, w1, w2):
    B, Tk, G, d = x_raw.shape
    r = CMP_LEN // CMP_STRIDE
    n_sub = Tk // CMP_STRIDE
    n_cmp = n_sub - r + 1
    sub = x_raw[:, :n_sub * CMP_STRIDE].reshape(B, n_sub, CMP_STRIDE, G, d)
    sub = jnp.moveaxis(sub, 3, 2).reshape(B, n_sub, G, CMP_STRIDE * d)
    parts = jnp.einsum('bngx,rxh->rbngh', sub, w1.reshape(r, CMP_STRIDE * d, -1))
    pre = parts[0, :, :n_cmp]
    for j in range(1, r):
        pre = pre + parts[j, :, j:j + n_cmp]
    pre = pre + pe.reshape(-1) @ w1
    return jax.nn.gelu(pre) @ w2


def _nsa_attend(q, q_pos0, kc, vc, ks, vs, kw, vw, kw_pos0, gates):
    B, Tq = q.shape[:2]
    QB = NSA_QBLK if Tq % NSA_QBLK == 0 else Tq
    nqb = Tq // QB
    scale = N_DQK ** -0.5
    n_cmp = kc.shape[1]
    cmp_start = jnp.arange(n_cmp) * CMP_STRIDE
    cmp_end = cmp_start + CMP_LEN - 1
    Tk = ks.shape[1]
    n_sel = -(-Tk // SEL_LEN)
    pad = n_sel * SEL_LEN - Tk
    ksb = jnp.moveaxis(jnp.pad(ks, ((0, 0), (0, pad), (0, 0), (0, 0))).reshape(B, n_sel, SEL_LEN, N_KV, N_DQK), 3, 1)
    vsb = jnp.moveaxis(jnp.pad(vs, ((0, 0), (0, pad), (0, 0), (0, 0))).reshape(B, n_sel, SEL_LEN, N_KV, N_DV), 3, 1)
    top = min(SEL_TOPK, n_sel)
    sel_start = jnp.arange(n_sel) * SEL_LEN
    overlap = ((cmp_start[:, None] <= sel_start[None, :] + SEL_LEN - 1) &
               (cmp_end[:, None] >= sel_start[None, :])).astype(F32)
    kwp = jnp.pad(kw, ((0, 0), (WINDOW, 0), (0, 0), (0, 0)))
    vwp = jnp.pad(vw, ((0, 0), (WINDOW, 0), (0, 0), (0, 0)))
    WK = WINDOW + QB
    qg = q.reshape(B, Tq, N_KV, N_HPG, N_DQK)
    gg = gates.reshape(B, Tq, N_KV, N_HPG, 3)
    gather_blocks = jax.vmap(jax.vmap(lambda blk, ix: blk[ix]))
    jj = jnp.arange(n_sel)

    def block(j):
        q0 = j * QB
        qb = lax.dynamic_slice_in_dim(qg, q0, QB, axis=1)
        gb = lax.dynamic_slice_in_dim(gg, q0, QB, axis=1)
        qpos = q_pos0 + q0 + jnp.arange(QB)
        s_c = jnp.einsum('bqghd,bngd->bghqn', qb, kc).astype(F32) * scale
        p_c = _masked_softmax(s_c, cmp_end[None, :] <= qpos[:, None])
        o_c = jnp.einsum('bghqn,bngd->bqghd', p_c.astype(vc.dtype), vc)
        imp = jnp.einsum('bghqn,nj->bgqj', p_c, overlap)
        cur = qpos // SEL_LEN
        valid = sel_start[None, :] <= qpos[:, None]
        forced = (jj[None, :] == 0) | ((jj[None, :] <= cur[:, None]) & (jj[None, :] > cur[:, None] - N_LOCAL))
        rank = jnp.where(forced, jnp.inf, jnp.where(valid, imp, -jnp.inf))
        _, idx = lax.top_k(rank, top)
        kg = gather_blocks(ksb, idx)
        vg = gather_blocks(vsb, idx).reshape(B, N_KV, QB, top * SEL_LEN, N_DV)
        kpos = (idx[..., None] * SEL_LEN + jnp.arange(SEL_LEN)).reshape(B, N_KV, QB, top * SEL_LEN)
        s_s = jnp.einsum('bqghd,bgqtkd->bghqtk', qb, kg).astype(F32) * scale
        s_s = s_s.reshape(B, N_KV, N_HPG, QB, top * SEL_LEN)
        p_s = _masked_softmax(s_s, (kpos <= qpos[None, None, :, None])[:, :, None])
        o_s = jnp.einsum('bghqx,bgqxd->bqghd', p_s.astype(vg.dtype), vg)
        start = q_pos0 + q0 - kw_pos0
        kbw = lax.dynamic_slice_in_dim(kwp, start, WK, axis=1)
        vbw = lax.dynamic_slice_in_dim(vwp, start, WK, axis=1)
        wpos = q_pos0 + q0 - WINDOW + jnp.arange(WK)
        wmask = ((wpos[None, :] <= qpos[:, None]) & (wpos[None, :] > qpos[:, None] - WINDOW) &
                 (wpos[None, :] >= kw_pos0))
        s_w = jnp.einsum('bqghd,bkgd->bghqk', qb, kbw).astype(F32) * scale
        p_w = _masked_softmax(s_w, wmask)
        o_w = jnp.einsum('bghqk,bkgd->bqghd', p_w.astype(vbw.dtype), vbw)
        out = gb[..., 0:1] * o_c + gb[..., 1:2] * o_s + gb[..., 2:3] * o_w
        return out.astype(q.dtype)

    outs = lax.map(block, jnp.arange(nqb))
    return jnp.moveaxis(outs, 0, 1).reshape(B, Tq, N_WIDTH)


def _mlstm_half(z, c0, n0, m0, prm, B, T):
    o = IN_OFFS
    mq = z[:, o[0]:o[1]].reshape(B, T, M_HEADS, M_DQK)
    mk = z[:, o[1]:o[2]].reshape(B, T, M_HEADS, M_DQK) * (M_DQK ** -0.5)
    mv = z[:, o[2]:o[3]].reshape(B, T, M_HEADS, M_DV)
    mo = z[:, o[3]:o[4]].reshape(B, T, M_WIDTH)
    mi = z[:, Z_G0:Z_G0 + M_HEADS].reshape(B, T, M_HEADS)
    mf = z[:, Z_G0 + M_HEADS:Z_G0 + 2 * M_HEADS].reshape(B, T, M_HEADS)
    ig = _softcap(mi + prm['b_mlstm_i'], GATE_CAP)
    lf = jax.nn.log_sigmoid(_softcap(mf + prm['b_mlstm_f'], GATE_CAP))
    h, c1, n1, m1 = _mlstm_mix(mq, mk, mv, ig, lf, c0, n0, m0)
    h = _rmsnorm(h, prm['mlstm_out_norm']).reshape(B, T, M_WIDTH)
    return h * jax.nn.sigmoid(mo), c1, n1, m1


def _gate_cols(z):
    return z[:, Z_G0 + 2 * M_HEADS:Z_G0 + 2 * M_HEADS + 3 * N_HEADS]


def _mix_fresh(x, prm, tm):
    B, T, _ = x.shape
    rows = B * T
    z = _matmul(x.reshape(rows, D_MODEL), prm['norm_mix'], prm['w_in_bf'], tm=tm, norm=True)
    c0 = jnp.zeros((B, M_HEADS, M_DQK, M_DV), F32)
    n0 = jnp.zeros((B, M_HEADS, M_DQK), F32)
    m0 = jnp.zeros((B, M_HEADS), F32)
    y_m, c1, n1, m1 = _mlstm_half(z, c0, n0, m0, prm, B, T)

    kvc = z[:, Z_KVC0:Z_KVC0 + KV_ROW].reshape(B, T, N_KV, N_KVW)
    kvs_rows, ks, vs = _kv_prep(z, Z_KVS0 // KV_ROW, prm['k_norm_slc'], tm=tm)
    kvw_rows, kw, vw = _kv_prep(z, Z_KVW0 // KV_ROW, prm['k_norm_win'], tm=tm)
    kc = _rmsnorm(_compress_blocks(kvc[..., :N_DQK], prm['cmp_pe_k'], prm['cmp_w1_k'], prm['cmp_w2_k']),
                  prm['k_norm_cmp'])
    vc = _compress_blocks(kvc[..., N_DQK:], prm['cmp_pe_v'], prm['cmp_w1_v'], prm['cmp_w2_v'])
    n_cmp = kc.shape[1]
    nc_pad = _round_up(n_cmp, LANE)
    kc_g = jnp.pad(jnp.moveaxis(kc, 2, 0), ((0, 0), (0, 0), (0, nc_pad - n_cmp), (0, Q_PAD - N_DQK)))
    vc_g = jnp.pad(jnp.moveaxis(vc, 2, 0), ((0, 0), (0, 0), (0, nc_pad - n_cmp), (0, 0)))
    kc_g = kc_g.astype(BF16).reshape(N_KV, B * nc_pad, Q_PAD)
    vc_g = vc_g.astype(BF16).reshape(N_KV, B * nc_pad, N_DV)
    graw = jnp.moveaxis(_gate_cols(z).reshape(rows, N_KV, 3 * N_HPG), 1, 0)
    graw = jnp.pad(graw, ((0, 0), (0, 0), (0, LANE - 3 * N_HPG)))
    y_n = _nsa_fresh(z, graw, ks, vs, kw, vw, kc_g, vc_g, prm['q_norm'], batch=B, t_len=T)

    mixed = jnp.concatenate([y_m.reshape(rows, M_WIDTH), y_n], axis=-1)
    x2 = x.reshape(rows, D_MODEL) + _matmul(mixed, None, prm['w_out_bf'], tm=tm, norm=False)
    keep = min(WINDOW, T)
    kv_win = kvw_rows.reshape(B, T, N_KV, N_KVW)[:, T - keep:]
    return x2, kvc, kvs_rows.reshape(B, T, N_KV, N_KVW), kv_win, c1, n1, m1


def _nsa_paged(q, gates, cache_cmp, cache_slc, win_buf, page_table, new_slc, new_win, prm):
    B, T = q.shape[:2]
    page = cache_cmp.shape[1]
    n_past = page_table.shape[1] * page
    assert T <= 8 and n_past % SEL_LEN == 0 and n_past % CMP_STRIDE == 0 and T < CMP_STRIDE
    assert (N_HPG * T * N_KV) % 8 == 0
    scale = N_DQK ** -0.5
    qpos = n_past + jnp.arange(T)
    qg = q.reshape(B, T, N_KV, N_HPG, N_DQK)
    gg = gates.reshape(B, T, N_KV, N_HPG, 3)

    pk, pv = _cmp_partial(cache_cmp, page_table, prm['cmp_w1_k'], prm['cmp_w1_v'])

    def finish(p, d, pe, w1, w2):
        pre = p[:, :, :-1, :d] + p[:, :, 1:, d:] + pe.reshape(-1) @ w1
        return jax.nn.gelu(pre) @ w2

    kc = _rmsnorm(finish(pk, N_DQK, prm['cmp_pe_k'], prm['cmp_w1_k'], prm['cmp_w2_k']), prm['k_norm_cmp'])
    vc = finish(pv, N_DV, prm['cmp_pe_v'], prm['cmp_w1_v'], prm['cmp_w2_v'])
    n_cmp = kc.shape[2]
    cmp_start = jnp.arange(n_cmp) * CMP_STRIDE
    cmp_end = cmp_start + CMP_LEN - 1
    s_c = jnp.einsum('bqghd,bgnd->bghqn', qg, kc).astype(F32) * scale
    p_c = _masked_softmax(s_c, cmp_end[None, :] <= qpos[:, None])
    o_c = jnp.einsum('bghqn,bgnd->bqghd', p_c, vc)

    n_sel = -(-(n_past + T) // SEL_LEN)
    top = min(SEL_TOPK, n_sel)
    sel_start = jnp.arange(n_sel) * SEL_LEN
    overlap = ((cmp_start[:, None] <= sel_start[None, :] + SEL_LEN - 1) &
               (cmp_end[:, None] >= sel_start[None, :])).astype(F32)
    imp = jnp.einsum('bghqn,nj->bgqj', p_c, overlap)
    jj = jnp.arange(n_sel)
    cur = qpos // SEL_LEN
    valid = sel_start[None, :] <= qpos[:, None]
    forced = (jj[None, :] == 0) | ((jj[None, :] <= cur[:, None]) & (jj[None, :] > cur[:, None] - N_LOCAL))
    rank = jnp.where(forced, jnp.inf, jnp.where(valid, imp, -jnp.inf))
    _, idx = lax.top_k(rank, top)
    chosen = jnp.any(idx[..., None] == jj, axis=-2)
    n_past_blk = n_past // SEL_LEN
    steps = page_table.shape[1] // PAGES_PER_STEP
    blk_step = n_past_blk // steps
    sel = jnp.broadcast_to(chosen[:, :, None, :, :n_past_blk], (B, N_KV, N_HPG, T, n_past_blk))
    sel = sel.reshape(B, N_KV * N_HPG * T, steps, blk_step)
    sel = jnp.pad(jnp.moveaxis(sel, 2, 1), ((0, 0), (0, 0), (0, 0), (0, LANE - blk_step))).astype(BF16)

    qx = jnp.transpose(qg, (0, 2, 3, 1, 4)).reshape(B, N_KV * N_HPG * T, N_DQK)
    pad8 = lambda a: jnp.pad(a, ((0, 0), (0, LANE - T), (0, 0), (0, 0)))
    o_s, o_w = _nsa_cached(cache_slc, page_table, qx.astype(BF16), sel, pad8(new_slc), win_buf, pad8(new_win),
                           t_new=T)
    to_q = lambda o: jnp.transpose(o.reshape(B, N_KV, N_HPG, T, N_DV), (0, 3, 1, 2, 4))
    out = gg[..., 0:1] * o_c + gg[..., 1:2] * to_q(o_s) + gg[..., 2:3] * to_q(o_w)
    return out.reshape(B, T, N_WIDTH)


def _mix_cached(x, cache_cmp, cache_slc, win_buf, page_table, c0, n0, m0, prm, tm):
    B, T, _ = x.shape
    rows = B * T
    z = _matmul(x.reshape(rows, D_MODEL), prm['norm_mix'], prm['w_in_bf'], tm=tm, norm=True)
    y_m, c1, n1, m1 = _mlstm_half(z, c0, n0, m0, prm, B, T)
    nq = z[:, Z_Q0:Z_KVC0].reshape(B, T, N_HEADS, Q_PAD)[..., :N_DQK]
    q = _rmsnorm(nq, prm['q_norm'])
    kvc = z[:, Z_KVC0:Z_KVC0 + KV_ROW].reshape(B, T, N_KV, N_KVW)
    kvs_rows, _, _ = _kv_prep(z, Z_KVS0 // KV_ROW, prm['k_norm_slc'], tm=tm)
    kvw_rows, _, _ = _kv_prep(z, Z_KVW0 // KV_ROW, prm['k_norm_win'], tm=tm)
    gates = jax.nn.sigmoid(_gate_cols(z)).reshape(B, T, N_HEADS, 3)
    y_n = _nsa_paged(q, gates, cache_cmp, cache_slc, win_buf, page_table,
                     kvs_rows.reshape(B, T, N_KV, N_KVW), kvw_rows.reshape(B, T, N_KV, N_KVW), prm)
    mixed = jnp.concatenate([y_m, y_n], axis=-1).reshape(rows, D_MODEL)
    x2 = x.reshape(rows, D_MODEL) + _matmul(mixed, None, prm['w_out_bf'], tm=tm, norm=False)
    n_past = page_table.shape[1] * cache_cmp.shape[1]
    keep = min(WINDOW, n_past + T)
    all_w = jnp.concatenate([win_buf, kvw_rows.reshape(B, T, N_KV, N_KVW)], axis=1)
    return (x2, kvc, kvs_rows.reshape(B, T, N_KV, N_KVW), all_w[:, all_w.shape[1] - keep:], c1, n1, m1)


def _ple_half(x3, ple, prm, tm):
    gate = _matmul(x3, prm['norm_ple'], prm['w_ple_gate_bf'], tm=tm, norm=True)
    proj = _matmul(ple, None, prm['w_ple_proj_bf'], tm=tm, norm=False)
    return x3 + jax.nn.sigmoid(gate) * proj


def kernel(x_prompt, x_sample, p_prompt, p_sample, cache_kv_cmp, cache_kv_slc, cache_kv_win, state_mlstm_C, state_mlstm_n, state_mlstm_m, page_table, norm_mix, w_in, b_mlstm_i, b_mlstm_f, mlstm_out_norm, q_norm, k_norm_cmp, k_norm_slc, k_norm_win, cmp_pe_k, cmp_w1_k, cmp_w2_k, cmp_pe_v, cmp_w1_v, cmp_w2_v, w_out, norm_ffn, w_router, b_router, w_gate, b_gate, w_up, b_up, w_down, b_down, norm_ple, w_ple_gate, w_ple_proj):
    depth = w_in.shape[0]
    assert depth == 1
    l = 0
    bp, sp, _ = x_prompt.shape
    db, ds, _ = x_sample.shape
    n_past = page_table.shape[1] * cache_kv_cmp.shape[2]
    prm = dict(norm_mix=norm_mix[l], b_mlstm_i=b_mlstm_i[l], b_mlstm_f=b_mlstm_f[l],
               mlstm_out_norm=mlstm_out_norm[l], q_norm=q_norm[l], k_norm_cmp=k_norm_cmp[l],
               k_norm_slc=k_norm_slc[l], k_norm_win=k_norm_win[l], cmp_pe_k=cmp_pe_k[l],
               cmp_w1_k=cmp_w1_k[l], cmp_w2_k=cmp_w2_k[l], cmp_pe_v=cmp_pe_v[l], cmp_w1_v=cmp_w1_v[l],
               cmp_w2_v=cmp_w2_v[l], norm_ffn=norm_ffn[l], w_router=w_router[l],
               b_router=b_router[l], w_gate=w_gate[l], b_gate=b_gate[l], w_up=w_up[l], b_up=b_up[l],
               w_down=w_down[l], b_down=b_down[l], norm_ple=norm_ple[l])
    prm['w_in_bf'] = _relayout_w_in(w_in[l])
    prm['w_out_bf'] = w_out[l].astype(BF16)
    prm['w_ple_gate_bf'] = w_ple_gate[l].astype(BF16)
    prm['w_ple_proj_bf'] = w_ple_proj[l].astype(BF16)

    tm_p, tm_s = 512, db * ds
    xp2, a1, a2, a3, a4, a5, a6 = _mix_fresh(x_prompt, prm, tm_p)
    xs2, b1, b2, b3, b4, b5, b6 = _mix_cached(x_sample, cache_kv_cmp[l], cache_kv_slc[l], cache_kv_win[l],
                                              page_table, state_mlstm_C[l], state_mlstm_n[l],
                                              state_mlstm_m[l], prm, tm_s)

    fp, lp = _router(xp2, prm['norm_ffn'], prm['w_router'], prm['b_router'], tm=tm_p)
    fs, ls = _router(xs2, prm['norm_ffn'], prm['w_router'], prm['b_router'], tm=tm_s)
    y_moe = _moe_ffn(jnp.concatenate([fp, fs], axis=0), jnp.concatenate([lp, ls], axis=0), prm)
    xp3 = xp2 + y_moe[:bp * sp]
    xs3 = xs2 + y_moe[bp * sp:]

    y_p = _ple_half(xp3, p_prompt[l].reshape(bp * sp, -1), prm, tm_p).reshape(bp, sp, D_MODEL)
    y_s = _ple_half(xs3, p_sample[l].reshape(db * ds, -1), prm, tm_s).reshape(db, ds, D_MODEL)

    st = lambda a: a[None]
    return (y_p, y_s, st(a1), st(a2), st(a3), st(a4), st(a5), st(a6),
            st(b1), st(b2), st(b3), st(b4), st(b5), st(b6))
```

```python
import functools

import numpy as np
import jax
import jax.numpy as jnp
from jax import lax
from jax.experimental import pallas as pl
from jax.experimental.pallas import tpu as pltpu

D_MODEL = 4096
M_HEADS = 4
M_DV = D_MODEL // (2 * M_HEADS)
M_DQK = M_DV // 2
M_WIDTH = M_HEADS * M_DV
M_CHUNK = 64
GATE_CAP = 15.0
N_HEADS = 16
N_KV = 4
N_HPG = N_HEADS // N_KV
N_DV = D_MODEL // (2 * N_HEADS)
N_DQK = 192
N_KVW = N_DQK + N_DV
N_WIDTH = N_HEADS * N_DV
CMP_LEN = 32
CMP_STRIDE = 16
SEL_LEN = 64
SEL_TOPK = 16
N_LOCAL = 2
WINDOW = 512
NSA_QBLK = 64
N_EXPERTS = 32
TOP_K = 4
SWIGLU_LIMIT = 7.0
SWIGLU_ALPHA = 1.702
EPS = 1e-6

IN_SIZES = (M_HEADS * M_DQK, M_HEADS * M_DQK, M_WIDTH, M_WIDTH, M_HEADS, M_HEADS,
            N_HEADS * N_DQK, N_KV * N_KVW, N_KV * N_KVW, N_KV * N_KVW, 3 * N_HEADS)
N_IN = sum(IN_SIZES)
IN_OFFS = tuple(int(v) for v in np.cumsum((0,) + IN_SIZES))

VMEM_LIMIT_BYTES = 56 * 1024 * 1024
LANE = 128
MM_TN = 512
MOE_TM = 256
MOE_TN = 512

BF16 = jnp.bfloat16
F32 = jnp.float32
NEG_INF = float("-inf")


def _round_up(a, b):
    return (a + b - 1) // b * b


Q_PAD = 256
Z_Q0 = IN_OFFS[4]
Z_KVC0 = Z_Q0 + N_HEADS * Q_PAD
Z_KVS0 = Z_KVC0 + N_KV * N_KVW
Z_KVW0 = Z_KVS0 + N_KV * N_KVW
Z_G0 = Z_KVW0 + N_KV * N_KVW
N_Z = _round_up(Z_G0 + LANE, MM_TN)
KV_ROW = N_KV * N_KVW
KV_PERM = N_KV * Q_PAD + N_KV * N_DV


def _dot(a, b):
    return jnp.dot(a, b, preferred_element_type=F32)


def _dot_nt(a, b):
    return lax.dot_general(a, b, (((1,), (1,)), ((), ())), preferred_element_type=F32)


def _split_bf16(a):
    hi = a.astype(BF16)
    return hi, (a - hi.astype(F32)).astype(BF16)


def _mm_kernel(a_ref, g_ref, w_ref, o_ref, a_bf, *, norm):
    @pl.when(pl.program_id(1) == 0)
    def _():
        a = a_ref[...]
        if norm:
            ms = jnp.mean(a * a, axis=-1, keepdims=True)
            a = a * lax.rsqrt(ms + EPS) * g_ref[...]
        a_bf[...] = a.astype(BF16)

    o_ref[...] = _dot(a_bf[...], w_ref[...])


def _matmul(a, gain, w_bf, *, tm, tn=MM_TN, norm):
    m, k = a.shape
    n = w_bf.shape[1]
    assert m % tm == 0 and n % tn == 0 and w_bf.shape[0] == k
    g = (jnp.ones((k,), F32) if gain is None else gain.astype(F32)).reshape(1, k)
    return pl.pallas_call(
        functools.partial(_mm_kernel, norm=norm),
        grid=(m // tm, n // tn),
        in_specs=[pl.BlockSpec((tm, k), lambda i, j: (i, 0)),
                  pl.BlockSpec((1, k), lambda i, j: (0, 0)),
                  pl.BlockSpec((k, tn), lambda i, j: (0, j))],
        out_specs=pl.BlockSpec((tm, tn), lambda i, j: (i, j)),
        out_shape=jax.ShapeDtypeStruct((m, n), F32),
        scratch_shapes=[pltpu.VMEM((tm, k), BF16)],
        compiler_params=pltpu.CompilerParams(
            dimension_semantics=("parallel", "arbitrary"),
            vmem_limit_bytes=VMEM_LIMIT_BYTES),
        name="dense_proj",
    )(a, g, w_bf)


def _relayout_w_in(w):
    d = w.shape[0]
    o = IN_OFFS
    wb = w.astype(BF16)
    cols = [wb[:, :o[4]]]
    zq = jnp.zeros((d, Q_PAD - N_DQK), BF16)
    for h in range(N_HEADS):
        cols += [wb[:, o[6] + h * N_DQK:o[6] + (h + 1) * N_DQK], zq]
    cols += [wb[:, o[7]:o[10]], wb[:, o[4]:o[6]], wb[:, o[10]:o[11]]]
    n_used = Z_G0 + (o[6] - o[4]) + (o[11] - o[10])
    cols += [jnp.zeros((d, N_Z - n_used), BF16)]
    return jnp.concatenate(cols, axis=1)


def _kv_prep_kernel(x_ref, gain_ref, vmask_ref, ind_ref, indt_ref, perm_ref, rows_ref, k_ref, v_ref):
    x = x_ref[...]
    sq_hi, sq_lo = _split_bf16(x * x)
    ss = _dot(sq_hi, ind_ref[...]) + _dot(sq_lo, ind_ref[...])
    inv_hi, inv_lo = _split_bf16(lax.rsqrt(ss * (1.0 / N_DQK) + EPS))
    scale = _dot(inv_hi, indt_ref[...]) + _dot(inv_lo, indt_ref[...]) + vmask_ref[...]
    y = x * scale * gain_ref[...]
    rows_ref[...] = y
    kv = _dot(y.astype(BF16), perm_ref[...])
    for g in range(N_KV):
        k_ref[g] = kv[:, g * Q_PAD:(g + 1) * Q_PAD].astype(BF16)
        v_ref[g] = kv[:, N_KV * Q_PAD + g * N_DV:N_KV * Q_PAD + (g + 1) * N_DV].astype(BF16)


def _kv_layout_constants():
    ind = np.zeros((KV_ROW, LANE), np.float32)
    vmask = np.zeros((1, KV_ROW), np.float32)
    perm = np.zeros((KV_ROW, KV_PERM), np.float32)
    for g in range(N_KV):
        ind[g * N_KVW:g * N_KVW + N_DQK, g] = 1.0
        vmask[0, g * N_KVW + N_DQK:(g + 1) * N_KVW] = 1.0
        for c in range(N_DQK):
            perm[g * N_KVW + c, g * Q_PAD + c] = 1.0
        for c in range(N_DV):
            perm[g * N_KVW + N_DQK + c, N_KV * Q_PAD + g * N_DV + c] = 1.0
    return ind, vmask, perm


def _kv_prep(z, col_block, k_gain, *, tm):
    rows = z.shape[0]
    ind, vmask, perm = _kv_layout_constants()
    gain_row = jnp.concatenate([k_gain.astype(F32), jnp.ones((N_DV,), F32)])
    gain_map = jnp.tile(gain_row, N_KV).reshape(1, KV_ROW)
    const = lambda shape: pl.BlockSpec(shape, lambda i: (0, 0))
    return pl.pallas_call(
        _kv_prep_kernel,
        grid=(rows // tm,),
        in_specs=[pl.BlockSpec((tm, KV_ROW), lambda i: (i, col_block)),
                  const((1, KV_ROW)), const((1, KV_ROW)), const((KV_ROW, LANE)),
                  const((LANE, KV_ROW)), const((KV_ROW, KV_PERM))],
        out_specs=[pl.BlockSpec((tm, KV_ROW), lambda i: (i, 0)),
                   pl.BlockSpec((N_KV, tm, Q_PAD), lambda i: (0, i, 0)),
                   pl.BlockSpec((N_KV, tm, N_DV), lambda i: (0, i, 0))],
        out_shape=[jax.ShapeDtypeStruct((rows, KV_ROW), F32),
                   jax.ShapeDtypeStruct((N_KV, rows, Q_PAD), BF16),
                   jax.ShapeDtypeStruct((N_KV, rows, N_DV), BF16)],
        compiler_params=pltpu.CompilerParams(
            dimension_semantics=("parallel",),
            vmem_limit_bytes=VMEM_LIMIT_BYTES),
        name="kv_prep",
    )(z, gain_map, jnp.asarray(vmask), jnp.asarray(ind, BF16), jnp.asarray(ind.T.copy(), BF16),
      jnp.asarray(perm, BF16))


def _exp_parts(s):
    mx = jnp.max(s, axis=-1, keepdims=True)
    mx = jnp.where(jnp.abs(mx) < jnp.inf, mx, 0.0)
    e = jnp.exp(s - mx)
    den = jnp.sum(e, axis=-1, keepdims=True)
    return e, jnp.where(den > 0, den, 1.0)


def _nsa_fresh_kernel(zq_ref, gt_ref, ks_ref, vs_ref, kw_ref, vw_ref, kc_ref, vc_ref,
                      qg_ref, ov_ref, ex_ref, o_ref, *, t_len, top, n_win):
    qb = NSA_QBLK
    q0 = pl.program_id(2) * qb
    scale = N_DQK ** -0.5
    n_sel = t_len // SEL_LEN

    parts = []
    for h in range(N_HPG):
        qh = zq_ref[:, h * Q_PAD:(h + 1) * Q_PAD]
        ms = jnp.sum(qh * qh, axis=-1, keepdims=True) * (1.0 / N_DQK)
        parts.append((qh * lax.rsqrt(ms + EPS) * qg_ref[...]).astype(BF16))
    qn = jnp.concatenate(parts, axis=0)
    qpos1 = q0 + lax.broadcasted_iota(jnp.int32, (qb, 1), 0)
    qpos = jnp.concatenate([qpos1] * N_HPG, axis=0)

    nc_pad = kc_ref.shape[1]
    cmp_end = lax.broadcasted_iota(jnp.int32, (1, nc_pad), 1) * CMP_STRIDE + (CMP_LEN - 1)
    s_c = jnp.where(cmp_end <= qpos, _dot_nt(qn, kc_ref[0]) * scale, NEG_INF)
    e_c, den_c = _exp_parts(s_c)
    p_c = e_c / den_c
    o_c = _dot(p_c.astype(BF16), vc_ref[0])

    psum = p_c[0:qb]
    for h in range(1, N_HPG):
        psum = psum + p_c[h * qb:(h + 1) * qb]
    ps_hi, ps_lo = _split_bf16(psum)
    imp = _dot(ps_hi, ov_ref[...]) + _dot(ps_lo, ov_ref[...])
    jl = lax.broadcasted_iota(jnp.int32, (1, LANE), 1)
    cur = qpos1 // SEL_LEN
    valid = jl * SEL_LEN <= qpos1
    forced = jnp.logical_or(jl == 0, jnp.logical_and(jl <= cur, jl > cur - N_LOCAL))
    rank = jnp.where(forced, jnp.inf, jnp.where(valid, imp, NEG_INF))
    ahead = jnp.zeros((qb, LANE), F32)
    for i in range(n_sel):
        ri = rank[:, i:i + 1]
        before = jnp.logical_or(ri > rank, jnp.logical_and(ri == rank, jl > i))
        ahead = ahead + jnp.where(before, 1.0, 0.0)
    sel = jnp.where(jnp.logical_and(ahead < top, valid), 1.0, 0.0).astype(BF16)
    selx = _dot(sel, ex_ref[...])
    kpos = lax.broadcasted_iota(jnp.int32, (1, t_len), 1)
    bias1 = jnp.where(jnp.logical_and(selx > 0.5, kpos <= qpos1), 0.0, NEG_INF)
    bias_s = jnp.concatenate([bias1] * N_HPG, axis=0)

    e_s, den_s = _exp_parts(_dot_nt(qn, ks_ref[0]) * scale + bias_s)
    o_s = _dot(e_s.astype(BF16), vs_ref[0]) / den_s

    ws = pl.multiple_of(jnp.maximum(q0 - WINDOW, 0), qb)
    wpos = ws + lax.broadcasted_iota(jnp.int32, (1, n_win), 1)
    mask_w = jnp.logical_and(wpos <= qpos, wpos > qpos - WINDOW)
    s_w = jnp.where(mask_w, _dot_nt(qn, kw_ref[0, pl.ds(ws, n_win), :]) * scale, NEG_INF)
    e_w, den_w = _exp_parts(s_w)
    o_w = _dot(e_w.astype(BF16), vw_ref[0, pl.ds(ws, n_win), :]) / den_w

    gt = jax.nn.sigmoid(gt_ref[0])
    for h in range(N_HPG):
        r = slice(h * qb, (h + 1) * qb)
        o_ref[:, h * N_DV:(h + 1) * N_DV] = (gt[:, 3 * h:3 * h + 1] * o_c[r]
                                             + gt[:, 3 * h + 1:3 * h + 2] * o_s[r]
                                             + gt[:, 3 * h + 2:3 * h + 3] * o_w[r])


def _nsa_fresh(z, graw, ks, vs, kw, vw, kc, vc, q_gain, *, batch, t_len, top=SEL_TOPK):
    qb = NSA_QBLK
    assert t_len % qb == 0 and t_len % SEL_LEN == 0 and SEL_LEN == qb
    nqb = t_len // qb
    n_sel = t_len // SEL_LEN
    n_cmp = t_len // CMP_STRIDE - CMP_LEN // CMP_STRIDE + 1
    nc_pad = kc.shape[1] // batch
    n_win = min(WINDOW + qb, t_len)
    cmp_start = np.arange(nc_pad) * CMP_STRIDE
    sel_start = np.arange(LANE) * SEL_LEN
    ov = ((cmp_start[:, None] <= sel_start[None, :] + SEL_LEN - 1) &
          (cmp_start[:, None] + CMP_LEN - 1 >= sel_start[None, :]) &
          (np.arange(nc_pad)[:, None] < n_cmp) & (np.arange(LANE)[None, :] < n_sel))
    ex = (np.arange(t_len)[None, :] // SEL_LEN) == np.arange(LANE)[:, None]
    qg = jnp.concatenate([q_gain.astype(F32), jnp.zeros((Q_PAD - N_DQK,), F32)]).reshape(1, Q_PAD)
    qw = N_HPG * Q_PAD
    return pl.pallas_call(
        functools.partial(_nsa_fresh_kernel, t_len=t_len, top=min(top, n_sel), n_win=n_win),
        grid=(batch, N_KV, nqb),
        in_specs=[pl.BlockSpec((qb, qw), lambda b, g, j: (b * nqb + j, Z_Q0 // qw + g)),
                  pl.BlockSpec((1, qb, LANE), lambda b, g, j: (g, b * nqb + j, 0)),
                  pl.BlockSpec((1, t_len, Q_PAD), lambda b, g, j: (g, b, 0)),
                  pl.BlockSpec((1, t_len, N_DV), lambda b, g, j: (g, b, 0)),
                  pl.BlockSpec((1, t_len, Q_PAD), lambda b, g, j: (g, b, 0)),
                  pl.BlockSpec((1, t_len, N_DV), lambda b, g, j: (g, b, 0)),
                  pl.BlockSpec((1, nc_pad, Q_PAD), lambda b, g, j: (g, b, 0)),
                  pl.BlockSpec((1, nc_pad, N_DV), lambda b, g, j: (g, b, 0)),
                  pl.BlockSpec((1, Q_PAD), lambda b, g, j: (0, 0)),
                  pl.BlockSpec((nc_pad, LANE), lambda b, g, j: (0, 0)),
                  pl.BlockSpec((LANE, t_len), lambda b, g, j: (0, 0))],
        out_specs=pl.BlockSpec((qb, N_HPG * N_DV), lambda b, g, j: (b * nqb + j, g)),
        out_shape=jax.ShapeDtypeStruct((batch * t_len, N_WIDTH), F32),
        compiler_params=pltpu.CompilerParams(
            dimension_semantics=("parallel", "parallel", "arbitrary"),
            vmem_limit_bytes=VMEM_LIMIT_BYTES),
        name="nsa_fresh",
    )(z, graw, ks, vs, kw, vw, kc, vc, qg, jnp.asarray(ov, BF16), jnp.asarray(ex, BF16))


PAGES_PER_STEP = 8
ROW_TILES = -(-N_KVW // LANE)


def _keys_minor(rows):
    return jnp.transpose(rows, (0, 2, 3, 1))


def _page_specs(page, n_pages):
    def spec(i):
        return pl.BlockSpec((1, N_KV, N_KVW, page),
                            lambda b, c, pt: (pt[b * n_pages + c * PAGES_PER_STEP + i], 0, 0, 0))
    return [spec(i) for i in range(PAGES_PER_STEP)]


def _cmp_partial_kernel(pt_ref, *refs, page):
    pages = refs[:PAGES_PER_STEP]
    wk0_ref, wk1_ref, wv1_ref, wv2_ref, pk_ref, pv_ref, stage = refs[PAGES_PER_STEP:]
    keys = PAGES_PER_STEP * page
    n_sub = keys // CMP_STRIDE
    tail = jnp.zeros((ROW_TILES * LANE - N_KVW, page), F32)
    for g in range(N_KV):
        for i, p in enumerate(pages):
            xt = p[0, g]
            r0 = g * keys + i * page
            stage[0, r0:r0 + page, :] = xt[:LANE].T
            stage[1, r0:r0 + page, :] = xt[LANE:2 * LANE].T
            stage[2, r0:r0 + page, :] = jnp.concatenate([xt[2 * LANE:], tail], axis=0).T
    acc_k = jnp.zeros((N_KV * n_sub, 2 * N_DQK), F32)
    acc_v = jnp.zeros((N_KV * n_sub, 2 * N_DV), F32)
    for t in range(CMP_STRIDE):
        a0, a1, a2 = [stage[j, pl.ds(t, N_KV * n_sub, stride=CMP_STRIDE), :].astype(BF16)
                      for j in range(ROW_TILES)]
        acc_k = acc_k + _dot(a0, wk0_ref[t]) + _dot(a1, wk1_ref[t])
        acc_v = acc_v + _dot(a1, wv1_ref[t]) + _dot(a2, wv2_ref[t])
    for g in range(N_KV):
        pk_ref[0, g] = acc_k[g * n_sub:(g + 1) * n_sub]
        pv_ref[0, g] = acc_v[g * n_sub:(g + 1) * n_sub]


def _cmp_partial(cache, page_table, w1_k, w1_v):
    n_pool, page = cache.shape[:2]
    batch, n_pages = page_table.shape
    assert page % CMP_STRIDE == 0 and n_pages % PAGES_PER_STEP == 0 and CMP_LEN == 2 * CMP_STRIDE
    assert ROW_TILES == 3 and LANE < N_DQK <= 2 * LANE < N_KVW
    sub_step = PAGES_PER_STEP * page // CMP_STRIDE
    n_sub = n_pages * page // CMP_STRIDE

    def halves(w1, d):
        w = w1.astype(BF16).reshape(2, CMP_STRIDE, d, d)
        return jnp.concatenate([w[0], w[1]], axis=-1)

    wk, wv = halves(w1_k, N_DQK), halves(w1_v, N_DV)
    zrow = lambda n, w: jnp.zeros((CMP_STRIDE, n, w.shape[-1]), BF16)
    wk0 = wk[:, :LANE]
    wk1 = jnp.concatenate([wk[:, LANE:], zrow(2 * LANE - N_DQK, wk)], axis=1)
    wv1 = jnp.concatenate([zrow(N_DQK - LANE, wv), wv[:, :2 * LANE - N_DQK]], axis=1)
    wv2 = jnp.concatenate([wv[:, 2 * LANE - N_DQK:], zrow(ROW_TILES * LANE - N_KVW, wv)], axis=1)

    const = lambda w: pl.BlockSpec(w.shape, lambda b, c, pt: (0, 0, 0))
    out = lambda d: pl.BlockSpec((1, N_KV, sub_step, 2 * d), lambda b, c, pt: (b, 0, c, 0))
    return pl.pallas_call(
        functools.partial(_cmp_partial_kernel, page=page),
        grid_spec=pltpu.PrefetchScalarGridSpec(
            num_scalar_prefetch=1,
            grid=(batch, n_pages // PAGES_PER_STEP),
            in_specs=_page_specs(page, n_pages) + [const(wk0), const(wk1), const(wv1), const(wv2)],
            out_specs=[out(N_DQK), out(N_DV)],
            scratch_shapes=[pltpu.VMEM((ROW_TILES, N_KV * PAGES_PER_STEP * page, LANE), F32)]),
        out_shape=[jax.ShapeDtypeStruct((batch, N_KV, n_sub, 2 * N_DQK), F32),
                   jax.ShapeDtypeStruct((batch, N_KV, n_sub, 2 * N_DV), F32)],
        compiler_params=pltpu.CompilerParams(
            dimension_semantics=("parallel", "arbitrary"),
            vmem_limit_bytes=VMEM_LIMIT_BYTES),
        name="cmp_partial",
    )(page_table.reshape(-1), *([_keys_minor(cache)] * PAGES_PER_STEP), wk0, wk1, wv1, wv2)


def _group_rows(refs, g):
    kt = jnp.concatenate([r[0, g, :N_DQK, :] for r in refs], axis=-1).astype(BF16)
    vt = jnp.concatenate([r[0, g, N_DQK:, :] for r in refs], axis=-1).astype(BF16)
    return kt, vt


def _group_scores(qx, krows):
    rg = qx.shape[0] // N_KV
    s = [_dot(qx[g * rg:(g + 1) * rg], krows[g][0]) for g in range(N_KV)]
    return jnp.concatenate(s, axis=0) * (N_DQK ** -0.5)


def _group_pv(p, krows):
    rg = p.shape[0] // N_KV
    return jnp.concatenate([_dot_nt(p[g * rg:(g + 1) * rg], krows[g][1]) for g in range(N_KV)], axis=0)


def _online_update(s, krows, m_sc, l_sc, acc_sc):
    m_old = m_sc[:, :1]
    m_new = jnp.maximum(m_old, jnp.max(s, axis=-1, keepdims=True))
    m_safe = jnp.where(jnp.abs(m_new) < jnp.inf, m_new, 0.0)
    alpha = jnp.exp(m_old - m_safe)
    p = jnp.exp(s - m_safe)
    l_sc[...] = jnp.broadcast_to(alpha * l_sc[:, :1] + jnp.sum(p, axis=-1, keepdims=True), l_sc.shape)
    acc_sc[...] = alpha * acc_sc[...] + _group_pv(p.astype(BF16), krows)
    m_sc[...] = jnp.broadcast_to(m_new, m_sc.shape)


def _nsa_cached_kernel(pt_ref, q_ref, sel_ref, ex_ref, news_ref, win_ref, neww_ref, *refs, t_new):
    pages = refs[:PAGES_PER_STEP]
    os_ref, ow_ref, m_sc, l_sc, acc_sc = refs[PAGES_PER_STEP:]
    c = pl.program_id(1)
    qx = q_ref[0]
    n_q = qx.shape[0]

    @pl.when(c == 0)
    def _():
        m_sc[...] = jnp.full(m_sc.shape, NEG_INF, F32)
        l_sc[...] = jnp.zeros(l_sc.shape, F32)
        acc_sc[...] = jnp.zeros(acc_sc.shape, F32)

    krows = [_group_rows(pages, g) for g in range(N_KV)]
    selx = _dot(sel_ref[0, 0], ex_ref[...])
    s = jnp.where(selx > 0.5, _group_scores(qx, krows), NEG_INF)
    _online_update(s, krows, m_sc, l_sc, acc_sc)

    @pl.when(c == pl.num_programs(1) - 1)
    def _():
        qi = lax.broadcasted_iota(jnp.int32, (n_q, 1), 0) % t_new
        nrows = [_group_rows([news_ref], g) for g in range(N_KV)]
        ki = lax.broadcasted_iota(jnp.int32, (1, news_ref.shape[3]), 1)
        s_n = jnp.where(ki <= qi, _group_scores(qx, nrows), NEG_INF)
        _online_update(s_n, nrows, m_sc, l_sc, acc_sc)
        os_ref[0] = acc_sc[...] / l_sc[:, :1]

        n_buf = win_ref.shape[3]
        wrows = [_group_rows([win_ref, neww_ref], g) for g in range(N_KV)]
        ri = lax.broadcasted_iota(jnp.int32, (1, n_buf + neww_ref.shape[3]), 1)
        vis = jnp.logical_or(jnp.logical_and(ri < n_buf, ri > qi + (n_buf - WINDOW)),
                             jnp.logical_and(ri >= n_buf, ri - n_buf <= qi))
        e_w, den_w = _exp_parts(jnp.where(vis, _group_scores(qx, wrows), NEG_INF))
        ow_ref[0] = _group_pv(e_w.astype(BF16), wrows) / den_w


def _nsa_cached(cache, page_table, qx, sel, new_slc, win_buf, new_win, *, t_new):
    page = cache.shape[1]
    batch, n_pages = page_table.shape
    n_q = qx.shape[1]
    keys_step = PAGES_PER_STEP * page
    assert keys_step % SEL_LEN == 0 and keys_step // SEL_LEN <= LANE and n_pages % PAGES_PER_STEP == 0
    assert win_buf.shape[1] >= WINDOW and t_new <= new_slc.shape[1]
    cache, new_slc, win_buf, new_win = map(_keys_minor, (cache, new_slc, win_buf, new_win))
    ex = (np.arange(keys_step)[None, :] // SEL_LEN) == np.arange(LANE)[:, None]
    per_b = lambda a: pl.BlockSpec((1,) + a.shape[1:], lambda b, c, pt: (b,) + (0,) * (a.ndim - 1))
    out = pl.BlockSpec((1, n_q, N_DV), lambda b, c, pt: (b, 0, 0))
    return pl.pallas_call(
        functools.partial(_nsa_cached_kernel, t_new=t_new),
        grid_spec=pltpu.PrefetchScalarGridSpec(
            num_scalar_prefetch=1,
            grid=(batch, n_pages // PAGES_PER_STEP),
            in_specs=[per_b(qx),
                      pl.BlockSpec((1, 1, n_q, LANE), lambda b, c, pt: (b, c, 0, 0)),
                      pl.BlockSpec((LANE, keys_step), lambda b, c, pt: (0, 0)),
                      per_b(new_slc), per_b(win_buf), per_b(new_win)]
            + _page_specs(page, n_pages),
            out_specs=[out, out],
            scratch_shapes=[pltpu.VMEM((n_q, LANE), F32), pltpu.VMEM((n_q, LANE), F32),
                            pltpu.VMEM((n_q, N_DV), F32)]),
        out_shape=[jax.ShapeDtypeStruct((batch, n_q, N_DV), F32)] * 2,
        compiler_params=pltpu.CompilerParams(
            dimension_semantics=("parallel", "arbitrary"),
            vmem_limit_bytes=VMEM_LIMIT_BYTES),
        name="nsa_cached",
    )(page_table.reshape(-1), qx, sel, jnp.asarray(ex, BF16), new_slc, win_buf, new_win,
      *([cache] * PAGES_PER_STEP))


def _router_kernel(x_ref, g_ref, whi_ref, wlo_ref, b_ref, f_ref, l_ref):
    x = x_ref[...]
    ms = jnp.mean(x * x, axis=-1, keepdims=True)
    f = x * lax.rsqrt(ms + EPS) * g_ref[...]
    f_hi, f_lo = _split_bf16(f)
    f_ref[...] = f_hi
    acc = _dot(f_hi, whi_ref[...]) + _dot(f_lo, whi_ref[...]) + _dot(f_hi, wlo_ref[...])
    l_ref[...] = acc + b_ref[...]


def _router(x, gain, w_router, b_router, *, tm):
    m, k = x.shape
    e = w_router.shape[1]
    w_hi = w_router.astype(BF16)
    w_lo = (w_router - w_hi.astype(F32)).astype(BF16)
    return pl.pallas_call(
        _router_kernel,
        grid=(m // tm,),
        in_specs=[pl.BlockSpec((tm, k), lambda i: (i, 0)),
                  pl.BlockSpec((1, k), lambda i: (0, 0)),
                  pl.BlockSpec((k, e), lambda i: (0, 0)),
                  pl.BlockSpec((k, e), lambda i: (0, 0)),
                  pl.BlockSpec((1, e), lambda i: (0, 0))],
        out_specs=[pl.BlockSpec((tm, k), lambda i: (i, 0)),
                   pl.BlockSpec((tm, e), lambda i: (i, 0))],
        out_shape=[jax.ShapeDtypeStruct((m, k), BF16),
                   jax.ShapeDtypeStruct((m, e), F32)],
        compiler_params=pltpu.CompilerParams(
            dimension_semantics=("parallel",),
            vmem_limit_bytes=VMEM_LIMIT_BYTES),
        name="moe_router",
    )(x, gain.reshape(1, k), w_hi, w_lo, b_router.reshape(1, e))


def _item_flags(blk_ref, exp_ref, st_ref, en_ref, tm):
    w = pl.program_id(1)
    e = exp_ref[w]
    b = blk_ref[w]
    prev = jnp.maximum(w - 1, 0)
    new_expert = jnp.logical_or(w == 0, exp_ref[prev] != e)
    new_block = jnp.logical_or(w == 0, blk_ref[prev] != b)
    rows = b * tm + lax.broadcasted_iota(jnp.int32, (tm, 1), 0)
    mask = jnp.logical_and(rows >= st_ref[e], rows < en_ref[e])
    return w, new_expert, new_block, mask


def _stream_expert_tiles(w, new_expert, exp_ref, nxt_ref, hbm_refs, land_refs, bf_refs, sem):
    tn = land_refs[0].shape[1]
    col = pl.multiple_of(pl.program_id(0) * tn, tn)

    def copies(e):
        return [pltpu.make_async_copy(hbm.at[e, :, pl.ds(col, tn)], land, sem.at[i])
                for i, (hbm, land) in enumerate(zip(hbm_refs, land_refs))]

    @pl.when(new_expert)
    def _():
        e = exp_ref[w]

        @pl.when(w == 0)
        def _():
            for c in copies(e):
                c.start()

        for c in copies(e):
            c.wait()
        for land, bf in zip(land_refs, bf_refs):
            bf[...] = land[...].astype(BF16)

        @pl.when(nxt_ref[w] >= 0)
        def _():
            for c in copies(nxt_ref[w]):
                c.start()


def _moe_up_kernel(blk_ref, exp_ref, st_ref, en_ref, tot_ref, nxt_ref,
                   x_ref, wg_hbm, wu_hbm, bg_ref, bu_ref, h_ref, wg_land, wu_land, wg_bf, wu_bf, sem, *, tm):
    w, new_expert, new_block, mask = _item_flags(blk_ref, exp_ref, st_ref, en_ref, tm)

    @pl.when(w < tot_ref[0])
    def _():
        _stream_expert_tiles(w, new_expert, exp_ref, nxt_ref, (wg_hbm, wu_hbm), (wg_land, wu_land),
                             (wg_bf, wu_bf), sem)
        x = x_ref[...]
        g = _dot(x, wg_bf[...]) + bg_ref[0]
        u = _dot(x, wu_bf[...]) + bu_ref[0]
        g = jnp.minimum(g, SWIGLU_LIMIT)
        u = jnp.clip(u, -SWIGLU_LIMIT, SWIGLU_LIMIT)
        hdn = (g * jax.nn.sigmoid(SWIGLU_ALPHA * g) * (u + 1.0)).astype(h_ref.dtype)
        keep = jnp.where(new_block, jnp.zeros_like(hdn), h_ref[...])
        h_ref[...] = jnp.where(mask, hdn, keep)


def _moe_down_kernel(blk_ref, exp_ref, st_ref, en_ref, tot_ref, nxt_ref,
                     h_ref, wd_hbm, bd_ref, y_ref, wd_land, wd_bf, sem, *, tm):
    w, new_expert, new_block, mask = _item_flags(blk_ref, exp_ref, st_ref, en_ref, tm)

    @pl.when(w < tot_ref[0])
    def _():
        _stream_expert_tiles(w, new_expert, exp_ref, nxt_ref, (wd_hbm,), (wd_land,), (wd_bf,), sem)
        y = _dot(h_ref[...], wd_bf[...]) + bd_ref[0]
        keep = jnp.where(new_block, jnp.zeros_like(y), y_ref[...])
        y_ref[...] = jnp.where(mask, y, keep)


def _moe_experts(x_sorted, sched, w_gate, b_gate, w_up, b_up, w_down, b_down):
    blk, exp, starts, ends, total, nxt = sched
    rows, d = x_sorted.shape
    n_e, _, d_h = w_gate.shape
    tm, tn = MOE_TM, MOE_TN
    n_items = blk.shape[0]
    params = pltpu.CompilerParams(dimension_semantics=("arbitrary", "arbitrary"),
                                  vmem_limit_bytes=VMEM_LIMIT_BYTES)

    row_spec = pl.BlockSpec((tm, d), lambda j, w, blk, exp, st, en, tot, nxt: (blk[w], 0))
    w_spec = pl.BlockSpec(memory_space=pl.ANY)
    b_spec = pl.BlockSpec((1, 1, tn), lambda j, w, blk, exp, st, en, tot, nxt: (exp[w], 0, j))
    out_spec = pl.BlockSpec((tm, tn), lambda j, w, blk, exp, st, en, tot, nxt: (blk[w], j))

    def tile_scratch(k_dim, n_w):
        return ([pltpu.VMEM((k_dim, tn), F32)] * n_w + [pltpu.VMEM((k_dim, tn), BF16)] * n_w
                + [pltpu.SemaphoreType.DMA((n_w,))])

    hidden = pl.pallas_call(
        functools.partial(_moe_up_kernel, tm=tm),
        grid_spec=pltpu.PrefetchScalarGridSpec(
            num_scalar_prefetch=6,
            grid=(d_h // tn, n_items),
            in_specs=[row_spec, w_spec, w_spec, b_spec, b_spec],
            out_specs=out_spec,
            scratch_shapes=tile_scratch(d, 2)),
        out_shape=jax.ShapeDtypeStruct((rows, d_h), BF16),
        compiler_params=params,
        name="moe_up",
    )(blk, exp, starts, ends, total, nxt, x_sorted, w_gate, w_up,
      b_gate.reshape(n_e, 1, d_h), b_up.reshape(n_e, 1, d_h))

    return pl.pallas_call(
        functools.partial(_moe_down_kernel, tm=tm),
        grid_spec=pltpu.PrefetchScalarGridSpec(
            num_scalar_prefetch=6,
            grid=(d // tn, n_items),
            in_specs=[row_spec, w_spec, b_spec],
            out_specs=out_spec,
            scratch_shapes=tile_scratch(d_h, 1)),
        out_shape=jax.ShapeDtypeStruct((rows, d), F32),
        compiler_params=params,
        name="moe_down",
    )(blk, exp, starts, ends, total, nxt, hidden, w_down, b_down.reshape(n_e, 1, d))


def _moe_schedule(top_e, tm):
    nk = top_e.size
    assert nk % tm == 0
    n_blocks = nk // tm
    n_items = n_blocks + N_EXPERTS - 1
    flat_e = top_e.reshape(nk)
    order = jnp.argsort(flat_e)
    counts = jnp.bincount(flat_e, length=N_EXPERTS).astype(jnp.int32)
    ends = jnp.cumsum(counts).astype(jnp.int32)
    starts = ends - counts
    first_blk = starts // tm
    last_blk = jnp.where(counts > 0, (ends - 1) // tm, first_blk - 1)
    n_it = last_blk - first_blk + 1
    it_end = jnp.cumsum(n_it).astype(jnp.int32)
    it_start = it_end - n_it
    total = it_end[-1]
    w = jnp.arange(n_items, dtype=jnp.int32)
    w_c = jnp.minimum(w, total - 1)
    e_w = jnp.minimum(jnp.searchsorted(it_end, w_c, side='right'), N_EXPERTS - 1).astype(jnp.int32)
    blk_w = (first_blk[e_w] + (w_c - it_start[e_w])).astype(jnp.int32)
    slot_of = jnp.zeros((nk,), jnp.int32).at[order].set(jnp.arange(nk, dtype=jnp.int32))
    ids = jnp.arange(N_EXPERTS, dtype=jnp.int32)
    later = lax.cummin(jnp.where(n_it > 0, ids, N_EXPERTS)[::-1])[::-1]
    nxt_e = jnp.concatenate([later[1:], jnp.full((1,), N_EXPERTS, jnp.int32)])
    nxt_e = jnp.where(nxt_e >= N_EXPERTS, -1, nxt_e).astype(jnp.int32)
    return order, slot_of, (blk_w, e_w, starts, ends, total.reshape(1), nxt_e[e_w])


def _moe_ffn(f_bf, logits, prm):
    n, d = f_bf.shape
    top_logit, top_e = lax.top_k(logits, TOP_K)
    gate = jax.nn.softmax(top_logit, axis=-1)
    order, slot_of, sched = _moe_schedule(top_e, MOE_TM)
    x_sorted = f_bf[order // TOP_K]
    y_sorted = _moe_experts(x_sorted, sched, prm['w_gate'], prm['b_gate'], prm['w_up'], prm['b_up'],
                            prm['w_down'], prm['b_down'])
    return jnp.einsum('nkd,nk->nd', y_sorted[slot_of].reshape(n, TOP_K, d), gate)


def _rmsnorm(x, g):
    xf = x.astype(F32)
    y = xf * lax.rsqrt(jnp.mean(xf * xf, axis=-1, keepdims=True) + EPS)
    return (y * g.astype(F32)).astype(x.dtype)


def _softcap(x, cap):
    return cap * jnp.tanh(x / cap)


def _masked_softmax(s, mask):
    s = jnp.where(mask, s.astype(F32), -jnp.inf)
    mx = jnp.max(s, axis=-1, keepdims=True)
    e = jnp.exp(s - jnp.where(jnp.isfinite(mx), mx, 0.0))
    den = jnp.sum(e, axis=-1, keepdims=True)
    return e / jnp.where(den > 0, den, 1.0)


def _mlstm_chunk(carry, inp):
    c, n, m = carry
    q, k, v, ig, lf = inp
    L = q.shape[2]
    b = jnp.cumsum(lf, axis=-1)
    causal = jnp.tril(jnp.ones((L, L), bool))
    dmat = jnp.where(causal, b[..., :, None] - b[..., None, :] + ig[..., None, :], -jnp.inf)
    inter = b + m[..., None]
    m_t = jnp.maximum(inter, jnp.max(dmat, axis=-1))
    w_intra = jnp.exp(dmat - m_t[..., None])
    w_inter = jnp.exp(inter - m_t)
    s = jnp.einsum('bhtd,bhsd->bhts', q, k) * w_intra
    num = jnp.einsum('bhts,bhsv->bhtv', s, v) + w_inter[..., None] * jnp.einsum('bhtd,bhdv->bhtv', q, c)
    den = jnp.sum(s, axis=-1) + w_inter * jnp.einsum('bhtd,bhd->bht', q, n)
    h = num / jnp.maximum(jnp.abs(den), jnp.exp(-m_t))[..., None]
    b_last = b[..., -1]
    g = b_last[..., None] - b + ig
    m_new = jnp.maximum(b_last + m, jnp.max(g, axis=-1))
    wk = jnp.exp(g - m_new[..., None])
    decay = jnp.exp(b_last + m - m_new)
    c_new = decay[..., None, None] * c + jnp.einsum('bhs,bhsd,bhsv->bhdv', wk, k, v)
    n_new = decay[..., None] * n + jnp.einsum('bhs,bhsd->bhd', wk, k)
    return (c_new, n_new, m_new), h


def _to_chunks(a, nc, L):
    a = a.astype(F32).reshape((a.shape[0], nc, L) + a.shape[2:])
    return jnp.moveaxis(jnp.moveaxis(a, 3, 2), 1, 0)


def _mlstm_mix(q, k, v, ig, lf, c0, n0, m0):
    B, T, H, _ = q.shape
    L = M_CHUNK if T % M_CHUNK == 0 else T
    nc = T // L
    xs = (_to_chunks(q, nc, L), _to_chunks(k, nc, L), _to_chunks(v, nc, L),
          _to_chunks(ig, nc, L), _to_chunks(lf, nc, L))
    init = (c0.astype(F32), n0.astype(F32), m0.astype(F32))
    (c1, n1, m1), hs = lax.scan(_mlstm_chunk, init, xs)
    h = jnp.moveaxis(jnp.moveaxis(hs, 0, 1), 2, 3).reshape(B, T, H, -1)
    return h, c1, n1, m1


def _compress_blocks(x_raw, pe, w1, w2):
    B, Tk, G, d = x_raw.shape
    r = CMP_LEN // CMP_STRIDE
    n_sub = Tk // CMP_STRIDE
    n_cmp = n_sub - r + 1
    sub = x_raw[:, :n_sub * CMP_STRIDE].reshape(B, n_sub, CMP_STRIDE, G, d)
    sub = jnp.moveaxis(sub, 3, 2).reshape(B, n_sub, G, CMP_STRIDE * d)
    parts = jnp.einsum('bngx,rxh->rbngh', sub, w1.reshape(r, CMP_STRIDE * d, -1))
    pre = parts[0, :, :n_cmp]
    for j in range(1, r):
        pre = pre + parts[j, :, j:j + n_cmp]
    pre = pre + pe.reshape(-1) @ w1
    return jax.nn.gelu(pre) @ w2


def _nsa_attend(q, q_pos0, kc, vc, ks, vs, kw, vw, kw_pos0, gates):
    B, Tq = q.shape[:2]
    QB = NSA_QBLK if Tq % NSA_QBLK == 0 else Tq
    nqb = Tq // QB
    scale = N_DQK ** -0.5
    n_cmp = kc.shape[1]
    cmp_start = jnp.arange(n_cmp) * CMP_STRIDE
    cmp_end = cmp_start + CMP_LEN - 1
    Tk = ks.shape[1]
    n_sel = -(-Tk // SEL_LEN)
    pad = n_sel * SEL_LEN - Tk
    ksb = jnp.moveaxis(jnp.pad(ks, ((0, 0), (0, pad), (0, 0), (0, 0))).reshape(B, n_sel, SEL_LEN, N_KV, N_DQK), 3, 1)
    vsb = jnp.moveaxis(jnp.pad(vs, ((0, 0), (0, pad), (0, 0), (0, 0))).reshape(B, n_sel, SEL_LEN, N_KV, N_DV), 3, 1)
    top = min(SEL_TOPK, n_sel)
    sel_start = jnp.arange(n_sel) * SEL_LEN
    overlap = ((cmp_start[:, None] <= sel_start[None, :] + SEL_LEN - 1) &
               (cmp_end[:, None] >= sel_start[None, :])).astype(F32)
    kwp = jnp.pad(kw, ((0, 0), (WINDOW, 0), (0, 0), (0, 0)))
    vwp = jnp.pad(vw, ((0, 0), (WINDOW, 0), (0, 0), (0, 0)))
    WK = WINDOW + QB
    qg = q.reshape(B, Tq, N_KV, N_HPG, N_DQK)
    gg = gates.reshape(B, Tq, N_KV, N_HPG, 3)
    gather_blocks = jax.vmap(jax.vmap(lambda blk, ix: blk[ix]))
    jj = jnp.arange(n_sel)

    def block(j):
        q0 = j * QB
        qb = lax.dynamic_slice_in_dim(qg, q0, QB, axis=1)
        gb = lax.dynamic_slice_in_dim(gg, q0, QB, axis=1)
        qpos = q_pos0 + q0 + jnp.arange(QB)
        s_c = jnp.einsum('bqghd,bngd->bghqn', qb, kc).astype(F32) * scale
        p_c = _masked_softmax(s_c, cmp_end[None, :] <= qpos[:, None])
        o_c = jnp.einsum('bghqn,bngd->bqghd', p_c.astype(vc.dtype), vc)
        imp = jnp.einsum('bghqn,nj->bgqj', p_c, overlap)
        cur = qpos // SEL_LEN
        valid = sel_start[None, :] <= qpos[:, None]
        forced = (jj[None, :] == 0) | ((jj[None, :] <= cur[:, None]) & (jj[None, :] > cur[:, None] - N_LOCAL))
        rank = jnp.where(forced, jnp.inf, jnp.where(valid, imp, -jnp.inf))
        _, idx = lax.top_k(rank, top)
        kg = gather_blocks(ksb, idx)
        vg = gather_blocks(vsb, idx).reshape(B, N_KV, QB, top * SEL_LEN, N_DV)
        kpos = (idx[..., None] * SEL_LEN + jnp.arange(SEL_LEN)).reshape(B, N_KV, QB, top * SEL_LEN)
        s_s = jnp.einsum('bqghd,bgqtkd->bghqtk', qb, kg).astype(F32) * scale
        s_s = s_s.reshape(B, N_KV, N_HPG, QB, top * SEL_LEN)
        p_s = _masked_softmax(s_s, (kpos <= qpos[None, None, :, None])[:, :, None])
        o_s = jnp.einsum('bghqx,bgqxd->bqghd', p_s.astype(vg.dtype), vg)
        start = q_pos0 + q0 - kw_pos0
        kbw = lax.dynamic_slice_in_dim(kwp, start, WK, axis=1)
        vbw = lax.dynamic_slice_in_dim(vwp, start, WK, axis=1)
        wpos = q_pos0 + q0 - WINDOW + jnp.arange(WK)
        wmask = ((wpos[None, :] <= qpos[:, None]) & (wpos[None, :] > qpos[:, None] - WINDOW) &
                 (wpos[None, :] >= kw_pos0))
        s_w = jnp.einsum('bqghd,bkgd->bghqk', qb, kbw).astype(F32) * scale
        p_w = _masked_softmax(s_w, wmask)
        o_w = jnp.einsum('bghqk,bkgd->bqghd', p_w.astype(vbw.dtype), vbw)
        out = gb[..., 0:1] * o_c + gb[..., 1:2] * o_s + gb[..., 2:3] * o_w
        return out.astype(q.dtype)

    outs = lax.map(block, jnp.arange(nqb))
    return jnp.moveaxis(outs, 0, 1).reshape(B, Tq, N_WIDTH)


def _mlstm_half(z, c0, n0, m0, prm, B, T):
    o = IN_OFFS
    mq = z[:, o[0]:o[1]].reshape(B, T, M_HEADS, M_DQK)
    mk = z[:, o[1]:o[2]].reshape(B, T, M_HEADS, M_DQK) * (M_DQK ** -0.5)
    mv = z[:, o[2]:o[3]].reshape(B, T, M_HEADS, M_DV)
    mo = z[:, o[3]:o[4]].reshape(B, T, M_WIDTH)
    mi = z[:, Z_G0:Z_G0 + M_HEADS].reshape(B, T, M_HEADS)
    mf = z[:, Z_G0 + M_HEADS:Z_G0 + 2 * M_HEADS].reshape(B, T, M_HEADS)
    ig = _softcap(mi + prm['b_mlstm_i'], GATE_CAP)
    lf = jax.nn.log_sigmoid(_softcap(mf + prm['b_mlstm_f'], GATE_CAP))
    h, c1, n1, m1 = _mlstm_mix(mq, mk, mv, ig, lf, c0, n0, m0)
    h = _rmsnorm(h, prm['mlstm_out_norm']).reshape(B, T, M_WIDTH)
    return h * jax.nn.sigmoid(mo), c1, n1, m1


def _gate_cols(z):
    return z[:, Z_G0 + 2 * M_HEADS:Z_G0 + 2 * M_HEADS + 3 * N_HEADS]


def _mix_fresh(x, prm, tm):
    B, T, _ = x.shape
    rows = B * T
    z = _matmul(x.reshape(rows, D_MODEL), prm['norm_mix'], prm['w_in_bf'], tm=tm, norm=True)
    c0 = jnp.zeros((B, M_HEADS, M_DQK, M_DV), F32)
    n0 = jnp.zeros((B, M_HEADS, M_DQK), F32)
    m0 = jnp.zeros((B, M_HEADS), F32)
    y_m, c1, n1, m1 = _mlstm_half(z, c0, n0, m0, prm, B, T)

    kvc = z[:, Z_KVC0:Z_KVC0 + KV_ROW].reshape(B, T, N_KV, N_KVW)
    kvs_rows, ks, vs = _kv_prep(z, Z_KVS0 // KV_ROW, prm['k_norm_slc'], tm=tm)
    kvw_rows, kw, vw = _kv_prep(z, Z_KVW0 // KV_ROW, prm['k_norm_win'], tm=tm)
    kc = _rmsnorm(_compress_blocks(kvc[..., :N_DQK], prm['cmp_pe_k'], prm['cmp_w1_k'], prm['cmp_w2_k']),
                  prm['k_norm_cmp'])
    vc = _compress_blocks(kvc[..., N_DQK:], prm['cmp_pe_v'], prm['cmp_w1_v'], prm['cmp_w2_v'])
    n_cmp = kc.shape[1]
    nc_pad = _round_up(n_cmp, LANE)
    kc_g = jnp.pad(jnp.moveaxis(kc, 2, 0), ((0, 0), (0, 0), (0, nc_pad - n_cmp), (0, Q_PAD - N_DQK)))
    vc_g = jnp.pad(jnp.moveaxis(vc, 2, 0), ((0, 0), (0, 0), (0, nc_pad - n_cmp), (0, 0)))
    kc_g = kc_g.astype(BF16).reshape(N_KV, B * nc_pad, Q_PAD)
    vc_g = vc_g.astype(BF16).reshape(N_KV, B * nc_pad, N_DV)
    graw = jnp.moveaxis(_gate_cols(z).reshape(rows, N_KV, 3 * N_HPG), 1, 0)
    graw = jnp.pad(graw, ((0, 0), (0, 0), (0, LANE - 3 * N_HPG)))
    y_n = _nsa_fresh(z, graw, ks, vs, kw, vw, kc_g, vc_g, prm['q_norm'], batch=B, t_len=T)

    mixed = jnp.concatenate([y_m.reshape(rows, M_WIDTH), y_n], axis=-1)
    x2 = x.reshape(rows, D_MODEL) + _matmul(mixed, None, prm['w_out_bf'], tm=tm, norm=False)
    keep = min(WINDOW, T)
    kv_win = kvw_rows.reshape(B, T, N_KV, N_KVW)[:, T - keep:]
    return x2, kvc, kvs_rows.reshape(B, T, N_KV, N_KVW), kv_win, c1, n1, m1


def _nsa_paged(q, gates, cache_cmp, cache_slc, win_buf, page_table, new_slc, new_win, prm):
    B, T = q.shape[:2]
    page = cache_cmp.shape[1]
    n_past = page_table.shape[1] * page
    assert T <= 8 and n_past % SEL_LEN == 0 and n_past % CMP_STRIDE == 0 and T < CMP_STRIDE
    assert (N_HPG * T * N_KV) % 8 == 0
    scale = N_DQK ** -0.5
    qpos = n_past + jnp.arange(T)
    qg = q.reshape(B, T, N_KV, N_HPG, N_DQK)
    gg = gates.reshape(B, T, N_KV, N_HPG, 3)

    pk, pv = _cmp_partial(cache_cmp, page_table, prm['cmp_w1_k'], prm['cmp_w1_v'])

    def finish(p, d, pe, w1, w2):
        pre = p[:, :, :-1, :d] + p[:, :, 1:, d:] + pe.reshape(-1) @ w1
        return jax.nn.gelu(pre) @ w2

    kc = _rmsnorm(finish(pk, N_DQK, prm['cmp_pe_k'], prm['cmp_w1_k'], prm['cmp_w2_k']), prm['k_norm_cmp'])
    vc = finish(pv, N_DV, prm['cmp_pe_v'], prm['cmp_w1_v'], prm['cmp_w2_v'])
    n_cmp = kc.shape[2]
    cmp_start = jnp.arange(n_cmp) * CMP_STRIDE
    cmp_end = cmp_start + CMP_LEN - 1
    s_c = jnp.einsum('bqghd,bgnd->bghqn', qg, kc).astype(F32) * scale
    p_c = _masked_softmax(s_c, cmp_end[None, :] <= qpos[:, None])
    o_c = jnp.einsum('bghqn,bgnd->bqghd', p_c, vc)

    n_sel = -(-(n_past + T) // SEL_LEN)
    top = min(SEL_TOPK, n_sel)
    sel_start = jnp.arange(n_sel) * SEL_LEN
    overlap = ((cmp_start[:, None] <= sel_start[None, :] + SEL_LEN - 1) &
               (cmp_end[:, None] >= sel_start[None, :])).astype(F32)
    imp = jnp.einsum('bghqn,nj->bgqj', p_c, overlap)
    jj = jnp.arange(n_sel)
    cur = qpos // SEL_LEN
    valid = sel_start[None, :] <= qpos[:, None]
    forced = (jj[None, :] == 0) | ((jj[None, :] <= cur[:, None]) & (jj[None, :] > cur[:, None] - N_LOCAL))
    rank = jnp.where(forced, jnp.inf, jnp.where(valid, imp, -jnp.inf))
    _, idx = lax.top_k(rank, top)
    chosen = jnp.any(idx[..., None] == jj, axis=-2)
    n_past_blk = n_past // SEL_LEN
    steps = page_table.shape[1] // PAGES_PER_STEP
    blk_step = n_past_blk // steps
    sel = jnp.broadcast_to(chosen[:, :, None, :, :n_past_blk], (B, N_KV, N_HPG, T, n_past_blk))
    sel = sel.reshape(B, N_KV * N_HPG * T, steps, blk_step)
    sel = jnp.pad(jnp.moveaxis(sel, 2, 1), ((0, 0), (0, 0), (0, 0), (0, LANE - blk_step))).astype(BF16)

    qx = jnp.transpose(qg, (0, 2, 3, 1, 4)).reshape(B, N_KV * N_HPG * T, N_DQK)
    pad8 = lambda a: jnp.pad(a, ((0, 0), (0, LANE - T), (0, 0), (0, 0)))
    o_s, o_w = _nsa_cached(cache_slc, page_table, qx.astype(BF16), sel, pad8(new_slc), win_buf, pad8(new_win),
                           t_new=T)
    to_q = lambda o: jnp.transpose(o.reshape(B, N_KV, N_HPG, T, N_DV), (0, 3, 1, 2, 4))
    out = gg[..., 0:1] * o_c + gg[..., 1:2] * to_q(o_s) + gg[..., 2:3] * to_q(o_w)
    return out.reshape(B, T, N_WIDTH)


def _mix_cached(x, cache_cmp, cache_slc, win_buf, page_table, c0, n0, m0, prm, tm):
    B, T, _ = x.shape
    rows = B * T
    z = _matmul(x.reshape(rows, D_MODEL), prm['norm_mix'], prm['w_in_bf'], tm=tm, norm=True)
    y_m, c1, n1, m1 = _mlstm_half(z, c0, n0, m0, prm, B, T)
    nq = z[:, Z_Q0:Z_KVC0].reshape(B, T, N_HEADS, Q_PAD)[..., :N_DQK]
    q = _rmsnorm(nq, prm['q_norm'])
    kvc = z[:, Z_KVC0:Z_KVC0 + KV_ROW].reshape(B, T, N_KV, N_KVW)
    kvs_rows, _, _ = _kv_prep(z, Z_KVS0 // KV_ROW, prm['k_norm_slc'], tm=tm)
    kvw_rows, _, _ = _kv_prep(z, Z_KVW0 // KV_ROW, prm['k_norm_win'], tm=tm)
    gates = jax.nn.sigmoid(_gate_cols(z)).reshape(B, T, N_HEADS, 3)
    y_n = _nsa_paged(q, gates, cache_cmp, cache_slc, win_buf, page_table,
                     kvs_rows.reshape(B, T, N_KV, N_KVW), kvw_rows.reshape(B, T, N_KV, N_KVW), prm)
    mixed = jnp.concatenate([y_m, y_n], axis=-1).reshape(rows, D_MODEL)
    x2 = x.reshape(rows, D_MODEL) + _matmul(mixed, None, prm['w_out_bf'], tm=tm, norm=False)
    n_past = page_table.shape[1] * cache_cmp.shape[1]
    keep = min(WINDOW, n_past + T)
    all_w = jnp.concatenate([win_buf, kvw_rows.reshape(B, T, N_KV, N_KVW)], axis=1)
    return (x2, kvc, kvs_rows.reshape(B, T, N_KV, N_KVW), all_w[:, all_w.shape[1] - keep:], c1, n1, m1)


def _ple_half(x3, ple, prm, tm):
    gate = _matmul(x3, prm['norm_ple'], prm['w_ple_gate_bf'], tm=tm, norm=True)
    proj = _matmul(ple, None, prm['w_ple_proj_bf'], tm=tm, norm=False)
    return x3 + jax.nn.sigmoid(gate) * proj


def kernel(x_prompt, x_sample, p_prompt, p_sample, cache_kv_cmp, cache_kv_slc, cache_kv_win, state_mlstm_C, state_mlstm_n, state_mlstm_m, page_table, norm_mix, w_in, b_mlstm_i, b_mlstm_f, mlstm_out_norm, q_norm, k_norm_cmp, k_norm_slc, k_norm_win, cmp_pe_k, cmp_w1_k, cmp_w2_k, cmp_pe_v, cmp_w1_v, cmp_w2_v, w_out, norm_ffn, w_router, b_router, w_gate, b_gate, w_up, b_up, w_down, b_down, norm_ple, w_ple_gate, w_ple_proj):
    depth = w_in.shape[0]
    assert depth == 1
    l = 0
    bp, sp, _ = x_prompt.shape
    db, ds, _ = x_sample.shape
    n_past = page_table.shape[1] * cache_kv_cmp.shape[2]
    prm = dict(norm_mix=norm_mix[l], b_mlstm_i=b_mlstm_i[l], b_mlstm_f=b_mlstm_f[l],
               mlstm_out_norm=mlstm_out_norm[l], q_norm=q_norm[l], k_norm_cmp=k_norm_cmp[l],
               k_norm_slc=k_norm_slc[l], k_norm_win=k_norm_win[l], cmp_pe_k=cmp_pe_k[l],
               cmp_w1_k=cmp_w1_k[l], cmp_w2_k=cmp_w2_k[l], cmp_pe_v=cmp_pe_v[l], cmp_w1_v=cmp_w1_v[l],
               cmp_w2_v=cmp_w2_v[l], norm_ffn=norm_ffn[l], w_router=w_router[l],
               b_router=b_router[l], w_gate=w_gate[l], b_gate=b_gate[l], w_up=w_up[l], b_up=b_up[l],
               w_down=w_down[l], b_down=b_down[l], norm_ple=norm_ple[l])
    prm['w_in_bf'] = _relayout_w_in(w_in[l])
    prm['w_out_bf'] = w_out[l].astype(BF16)
    prm['w_ple_gate_bf'] = w_ple_gate[l].astype(BF16)
    prm['w_ple_proj_bf'] = w_ple_proj[l].astype(BF16)

    tm_p, tm_s = 512, db * ds
    xp2, a1, a2, a3, a4, a5, a6 = _mix_fresh(x_prompt, prm, tm_p)
    xs2, b1, b2, b3, b4, b5, b6 = _mix_cached(x_sample, cache_kv_cmp[l], cache_kv_slc[l], cache_kv_win[l],
                                              page_table, state_mlstm_C[l], state_mlstm_n[l],
                                              state_mlstm_m[l], prm, tm_s)

    fp, lp = _router(xp2, prm['norm_ffn'], prm['w_router'], prm['b_router'], tm=tm_p)
    fs, ls = _router(xs2, prm['norm_ffn'], prm['w_router'], prm['b_router'], tm=tm_s)
    y_moe = _moe_ffn(jnp.concatenate([fp, fs], axis=0), jnp.concatenate([lp, ls], axis=0), prm)
    xp3 = xp2 + y_moe[:bp * sp]
    xs3 = xs2 + y_moe[bp * sp:]

    y_p = _ple_half(xp3, p_prompt[l].reshape(bp * sp, -1), prm, tm_p).reshape(bp, sp, D_MODEL)
    y_s = _ple_half(xs3, p_sample[l].reshape(db * ds, -1), prm, tm_s).reshape(db, ds, D_MODEL)

    st = lambda a: a[None]
    return (y_p, y_s, st(a1), st(a2), st(a3), st(a4), st(a5), st(a6),
            st(b1), st(b2), st(b3), st(b4), st(b5), st(b6))
```

```python
import functools

import numpy as np
import jax
import jax.numpy as jnp
from jax import lax
from jax.experimental import pallas as pl
from jax.experimental.pallas import tpu as pltpu

D_MODEL = 4096
M_HEADS = 4
M_DV = D_MODEL // (2 * M_HEADS)
M_DQK = M_DV // 2
M_WIDTH = M_HEADS * M_DV
M_CHUNK = 64
GATE_CAP = 15.0
N_HEADS = 16
N_KV = 4
N_HPG = N_HEADS // N_KV
N_DV = D_MODEL // (2 * N_HEADS)
N_DQK = 192
N_KVW = N_DQK + N_DV
N_WIDTH = N_HEADS * N_DV
CMP_LEN = 32
CMP_STRIDE = 16
SEL_LEN = 64
SEL_TOPK = 16
N_LOCAL = 2
WINDOW = 512
NSA_QBLK = 64
N_EXPERTS = 32
TOP_K = 4
SWIGLU_LIMIT = 7.0
SWIGLU_ALPHA = 1.702
EPS = 1e-6

IN_SIZES = (M_HEADS * M_DQK, M_HEADS * M_DQK, M_WIDTH, M_WIDTH, M_HEADS, M_HEADS,
            N_HEADS * N_DQK, N_KV * N_KVW, N_KV * N_KVW, N_KV * N_KVW, 3 * N_HEADS)
N_IN = sum(IN_SIZES)
IN_OFFS = tuple(int(v) for v in np.cumsum((0,) + IN_SIZES))

VMEM_LIMIT_BYTES = 56 * 1024 * 1024
LANE = 128
MM_TN = 512
MOE_TM = 256
MOE_TN = 512

BF16 = jnp.bfloat16
F32 = jnp.float32
NEG_INF = float("-inf")


def _round_up(a, b):
    return (a + b - 1) // b * b


Q_PAD = 256
Z_Q0 = IN_OFFS[4]
Z_KVC0 = Z_Q0 + N_HEADS * Q_PAD
Z_KVS0 = Z_KVC0 + N_KV * N_KVW
Z_KVW0 = Z_KVS0 + N_KV * N_KVW
Z_G0 = Z_KVW0 + N_KV * N_KVW
N_Z = _round_up(Z_G0 + LANE, MM_TN)
KV_ROW = N_KV * N_KVW
KV_PERM = N_KV * Q_PAD + N_KV * N_DV


def _dot(a, b):
    return jnp.dot(a, b, preferred_element_type=F32)


def _dot_nt(a, b):
    return lax.dot_general(a, b, (((1,), (1,)), ((), ())), preferred_element_type=F32)


def _split_bf16(a):
    hi = a.astype(BF16)
    return hi, (a - hi.astype(F32)).astype(BF16)


def _mm_kernel(a_ref, g_ref, w_ref, o_ref, a_bf, *, norm):
    @pl.when(pl.program_id(1) == 0)
    def _():
        a = a_ref[...]
        if norm:
            ms = jnp.mean(a * a, axis=-1, keepdims=True)
            a = a * lax.rsqrt(ms + EPS) * g_ref[...]
        a_bf[...] = a.astype(BF16)

    o_ref[...] = _dot(a_bf[...], w_ref[...])


def _matmul(a, gain, w_bf, *, tm, tn=MM_TN, norm):
    m, k = a.shape
    n = w_bf.shape[1]
    assert m % tm == 0 and n % tn == 0 and w_bf.shape[0] == k
    g = (jnp.ones((k,), F32) if gain is None else gain.astype(F32)).reshape(1, k)
    return pl.pallas_call(
        functools.partial(_mm_kernel, norm=norm),
        grid=(m // tm, n // tn),
        in_specs=[pl.BlockSpec((tm, k), lambda i, j: (i, 0)),
                  pl.BlockSpec((1, k), lambda i, j: (0, 0)),
                  pl.BlockSpec((k, tn), lambda i, j: (0, j))],
        out_specs=pl.BlockSpec((tm, tn), lambda i, j: (i, j)),
        out_shape=jax.ShapeDtypeStruct((m, n), F32),
        scratch_shapes=[pltpu.VMEM((tm, k), BF16)],
        compiler_params=pltpu.CompilerParams(
            dimension_semantics=("parallel", "arbitrary"),
            vmem_limit_bytes=VMEM_LIMIT_BYTES),
        name="dense_proj",
    )(a, g, w_bf)


def _relayout_w_in(w):
    d = w.shape[0]
    o = IN_OFFS
    wb = w.astype(BF16)
    cols = [wb[:, :o[4]]]
    zq = jnp.zeros((d, Q_PAD - N_DQK), BF16)
    for h in range(N_HEADS):
        cols += [wb[:, o[6] + h * N_DQK:o[6] + (h + 1) * N_DQK], zq]
    cols += [wb[:, o[7]:o[10]], wb[:, o[4]:o[6]], wb[:, o[10]:o[11]]]
    n_used = Z_G0 + (o[6] - o[4]) + (o[11] - o[10])
    cols += [jnp.zeros((d, N_Z - n_used), BF16)]
    return jnp.concatenate(cols, axis=1)


def _kv_prep_kernel(x_ref, gain_ref, vmask_ref, ind_ref, indt_ref, perm_ref, rows_ref, k_ref, v_ref):
    x = x_ref[...]
    sq_hi, sq_lo = _split_bf16(x * x)
    ss = _dot(sq_hi, ind_ref[...]) + _dot(sq_lo, ind_ref[...])
    inv_hi, inv_lo = _split_bf16(lax.rsqrt(ss * (1.0 / N_DQK) + EPS))
    scale = _dot(inv_hi, indt_ref[...]) + _dot(inv_lo, indt_ref[...]) + vmask_ref[...]
    y = x * scale * gain_ref[...]
    rows_ref[...] = y
    kv = _dot(y.astype(BF16), perm_ref[...])
    for g in range(N_KV):
        k_ref[g] = kv[:, g * Q_PAD:(g + 1) * Q_PAD].astype(BF16)
        v_ref[g] = kv[:, N_KV * Q_PAD + g * N_DV:N_KV * Q_PAD + (g + 1) * N_DV].astype(BF16)


def _kv_layout_constants():
    ind = np.zeros((KV_ROW, LANE), np.float32)
    vmask = np.zeros((1, KV_ROW), np.float32)
    perm = np.zeros((KV_ROW, KV_PERM), np.float32)
    for g in range(N_KV):
        ind[g * N_KVW:g * N_KVW + N_DQK, g] = 1.0
        vmask[0, g * N_KVW + N_DQK:(g + 1) * N_KVW] = 1.0
        for c in range(N_DQK):
            perm[g * N_KVW + c, g * Q_PAD + c] = 1.0
        for c in range(N_DV):
            perm[g * N_KVW + N_DQK + c, N_KV * Q_PAD + g * N_DV + c] = 1.0
    return ind, vmask, perm


def _kv_prep(z, col_block, k_gain, *, tm):
    rows = z.shape[0]
    ind, vmask, perm = _kv_layout_constants()
    gain_row = jnp.concatenate([k_gain.astype(F32), jnp.ones((N_DV,), F32)])
    gain_map = jnp.tile(gain_row, N_KV).reshape(1, KV_ROW)
    const = lambda shape: pl.BlockSpec(shape, lambda i: (0, 0))
    return pl.pallas_call(
        _kv_prep_kernel,
        grid=(rows // tm,),
        in_specs=[pl.BlockSpec((tm, KV_ROW), lambda i: (i, col_block)),
                  const((1, KV_ROW)), const((1, KV_ROW)), const((KV_ROW, LANE)),
                  const((LANE, KV_ROW)), const((KV_ROW, KV_PERM))],
        out_specs=[pl.BlockSpec((tm, KV_ROW), lambda i: (i, 0)),
                   pl.BlockSpec((N_KV, tm, Q_PAD), lambda i: (0, i, 0)),
                   pl.BlockSpec((N_KV, tm, N_DV), lambda i: (0, i, 0))],
        out_shape=[jax.ShapeDtypeStruct((rows, KV_ROW), F32),
                   jax.ShapeDtypeStruct((N_KV, rows, Q_PAD), BF16),
                   jax.ShapeDtypeStruct((N_KV, rows, N_DV), BF16)],
        compiler_params=pltpu.CompilerParams(
            dimension_semantics=("parallel",),
            vmem_limit_bytes=VMEM_LIMIT_BYTES),
        name="kv_prep",
    )(z, gain_map, jnp.asarray(vmask), jnp.asarray(ind, BF16), jnp.asarray(ind.T.copy(), BF16),
      jnp.asarray(perm, BF16))


def _exp_parts(s):
    mx = jnp.max(s, axis=-1, keepdims=True)
    mx = jnp.where(jnp.abs(mx) < jnp.inf, mx, 0.0)
    e = jnp.exp(s - mx)
    den = jnp.sum(e, axis=-1, keepdims=True)
    return e, jnp.where(den > 0, den, 1.0)


CAUSAL_SPLITS = 4


def _nsa_fresh_kernel(zq_ref, gt_ref, ks_ref, vs_ref, kw_ref, vw_ref, kc_ref, vc_ref,
                      qg_ref, ov_ref, ex_ref, o_ref, os_sc, *, t_len, top, n_win):
    qb = NSA_QBLK
    q0 = pl.program_id(2) * qb
    scale = N_DQK ** -0.5
    n_sel = t_len // SEL_LEN

    parts = []
    for h in range(N_HPG):
        qh = zq_ref[:, h * Q_PAD:(h + 1) * Q_PAD]
        ms = jnp.sum(qh * qh, axis=-1, keepdims=True) * (1.0 / N_DQK)
        parts.append((qh * lax.rsqrt(ms + EPS) * qg_ref[...]).astype(BF16))
    qn = jnp.concatenate(parts, axis=0)
    qpos1 = q0 + lax.broadcasted_iota(jnp.int32, (qb, 1), 0)
    qpos = jnp.concatenate([qpos1] * N_HPG, axis=0)

    nc_pad = kc_ref.shape[1]
    cmp_end = lax.broadcasted_iota(jnp.int32, (1, nc_pad), 1) * CMP_STRIDE + (CMP_LEN - 1)
    s_c = jnp.where(cmp_end <= qpos, _dot_nt(qn, kc_ref[0]) * scale, NEG_INF)
    e_c, den_c = _exp_parts(s_c)
    p_c = e_c / den_c
    o_c = _dot(p_c.astype(BF16), vc_ref[0])

    psum = p_c[0:qb]
    for h in range(1, N_HPG):
        psum = psum + p_c[h * qb:(h + 1) * qb]
    ps_hi, ps_lo = _split_bf16(psum)
    imp = _dot(ps_hi, ov_ref[...]) + _dot(ps_lo, ov_ref[...])
    jl = lax.broadcasted_iota(jnp.int32, (1, LANE), 1)
    cur = qpos1 // SEL_LEN
    valid = jl * SEL_LEN <= qpos1
    forced = jnp.logical_or(jl == 0, jnp.logical_and(jl <= cur, jl > cur - N_LOCAL))
    rank = jnp.where(forced, jnp.inf, jnp.where(valid, imp, NEG_INF))
    ahead = jnp.zeros((qb, LANE), F32)
    for i in range(n_sel):
        ri = rank[:, i:i + 1]
        before = jnp.logical_or(ri > rank, jnp.logical_and(ri == rank, jl > i))
        ahead = ahead + jnp.where(before, 1.0, 0.0)
    sel = jnp.where(jnp.logical_and(ahead < top, valid), 1.0, 0.0).astype(BF16)
    selx = _dot(sel, ex_ref[...])
    kpos = lax.broadcasted_iota(jnp.int32, (1, t_len), 1)
    bias1 = jnp.where(jnp.logical_and(selx > 0.5, kpos <= qpos1), 0.0, NEG_INF)
    bias_s = jnp.concatenate([bias1] * N_HPG, axis=0)

    span = t_len // CAUSAL_SPLITS
    part = (q0 + qb - 1) // span
    for i in range(CAUSAL_SPLITS):
        nk = (i + 1) * span

        @pl.when(part == i)
        def _(nk=nk):
            e_s, den_s = _exp_parts(_dot_nt(qn, ks_ref[0, :nk, :]) * scale + bias_s[:, :nk])
            os_sc[...] = _dot(e_s.astype(BF16), vs_ref[0, :nk, :]) / den_s

    o_s = os_sc[...]

    ws = pl.multiple_of(jnp.maximum(q0 - WINDOW, 0), qb)
    wpos = ws + lax.broadcasted_iota(jnp.int32, (1, n_win), 1)
    mask_w = jnp.logical_and(wpos <= qpos, wpos > qpos - WINDOW)
    s_w = jnp.where(mask_w, _dot_nt(qn, kw_ref[0, pl.ds(ws, n_win), :]) * scale, NEG_INF)
    e_w, den_w = _exp_parts(s_w)
    o_w = _dot(e_w.astype(BF16), vw_ref[0, pl.ds(ws, n_win), :]) / den_w

    gt = jax.nn.sigmoid(gt_ref[0])
    for h in range(N_HPG):
        r = slice(h * qb, (h + 1) * qb)
        o_ref[:, h * N_DV:(h + 1) * N_DV] = (gt[:, 3 * h:3 * h + 1] * o_c[r]
                                             + gt[:, 3 * h + 1:3 * h + 2] * o_s[r]
                                             + gt[:, 3 * h + 2:3 * h + 3] * o_w[r])


def _nsa_fresh(z, graw, ks, vs, kw, vw, kc, vc, q_gain, *, batch, t_len, top=SEL_TOPK):
    qb = NSA_QBLK
    assert t_len % qb == 0 and t_len % SEL_LEN == 0 and SEL_LEN == qb and t_len % (CAUSAL_SPLITS * LANE) == 0
    nqb = t_len // qb
    n_sel = t_len // SEL_LEN
    n_cmp = t_len // CMP_STRIDE - CMP_LEN // CMP_STRIDE + 1
    nc_pad = kc.shape[1] // batch
    n_win = min(WINDOW + qb, t_len)
    cmp_start = np.arange(nc_pad) * CMP_STRIDE
    sel_start = np.arange(LANE) * SEL_LEN
    ov = ((cmp_start[:, None] <= sel_start[None, :] + SEL_LEN - 1) &
          (cmp_start[:, None] + CMP_LEN - 1 >= sel_start[None, :]) &
          (np.arange(nc_pad)[:, None] < n_cmp) & (np.arange(LANE)[None, :] < n_sel))
    ex = (np.arange(t_len)[None, :] // SEL_LEN) == np.arange(LANE)[:, None]
    qg = jnp.concatenate([q_gain.astype(F32), jnp.zeros((Q_PAD - N_DQK,), F32)]).reshape(1, Q_PAD)
    qw = N_HPG * Q_PAD
    return pl.pallas_call(
        functools.partial(_nsa_fresh_kernel, t_len=t_len, top=min(top, n_sel), n_win=n_win),
        grid=(batch, N_KV, nqb),
        in_specs=[pl.BlockSpec((qb, qw), lambda b, g, j: (b * nqb + j, Z_Q0 // qw + g)),
                  pl.BlockSpec((1, qb, LANE), lambda b, g, j: (g, b * nqb + j, 0)),
                  pl.BlockSpec((1, t_len, Q_PAD), lambda b, g, j: (g, b, 0)),
                  pl.BlockSpec((1, t_len, N_DV), lambda b, g, j: (g, b, 0)),
                  pl.BlockSpec((1, t_len, Q_PAD), lambda b, g, j: (g, b, 0)),
                  pl.BlockSpec((1, t_len, N_DV), lambda b, g, j: (g, b, 0)),
                  pl.BlockSpec((1, nc_pad, Q_PAD), lambda b, g, j: (g, b, 0)),
                  pl.BlockSpec((1, nc_pad, N_DV), lambda b, g, j: (g, b, 0)),
                  pl.BlockSpec((1, Q_PAD), lambda b, g, j: (0, 0)),
                  pl.BlockSpec((nc_pad, LANE), lambda b, g, j: (0, 0)),
                  pl.BlockSpec((LANE, t_len), lambda b, g, j: (0, 0))],
        out_specs=pl.BlockSpec((qb, N_HPG * N_DV), lambda b, g, j: (b * nqb + j, g)),
        out_shape=jax.ShapeDtypeStruct((batch * t_len, N_WIDTH), F32),
        scratch_shapes=[pltpu.VMEM((N_HPG * qb, N_DV), F32)],
        compiler_params=pltpu.CompilerParams(
            dimension_semantics=("parallel", "parallel", "arbitrary"),
            vmem_limit_bytes=VMEM_LIMIT_BYTES),
        name="nsa_fresh",
    )(z, graw, ks, vs, kw, vw, kc, vc, qg, jnp.asarray(ov, BF16), jnp.asarray(ex, BF16))


PAGES_PER_STEP = 8
ROW_TILES = -(-N_KVW // LANE)


def _keys_minor(rows):
    return jnp.transpose(rows, (0, 2, 3, 1))


def _page_specs(page, n_pages):
    def spec(i):
        return pl.BlockSpec((1, N_KV, N_KVW, page),
                            lambda b, c, pt: (pt[b * n_pages + c * PAGES_PER_STEP + i], 0, 0, 0))
    return [spec(i) for i in range(PAGES_PER_STEP)]


def _cmp_partial_kernel(pt_ref, *refs, page):
    pages = refs[:PAGES_PER_STEP]
    wk0_ref, wk1_ref, wv1_ref, wv2_ref, pk_ref, pv_ref, stage = refs[PAGES_PER_STEP:]
    keys = PAGES_PER_STEP * page
    n_sub = keys // CMP_STRIDE
    tail = jnp.zeros((ROW_TILES * LANE - N_KVW, page), F32)
    for g in range(N_KV):
        for i, p in enumerate(pages):
            xt = p[0, g]
            r0 = g * keys + i * page
            stage[0, r0:r0 + page, :] = xt[:LANE].T
            stage[1, r0:r0 + page, :] = xt[LANE:2 * LANE].T
            stage[2, r0:r0 + page, :] = jnp.concatenate([xt[2 * LANE:], tail], axis=0).T
    acc_k = jnp.zeros((N_KV * n_sub, 2 * N_DQK), F32)
    acc_v = jnp.zeros((N_KV * n_sub, 2 * N_DV), F32)
    for t in range(CMP_STRIDE):
        a0, a1, a2 = [stage[j, pl.ds(t, N_KV * n_sub, stride=CMP_STRIDE), :].astype(BF16)
                      for j in range(ROW_TILES)]
        acc_k = acc_k + _dot(a0, wk0_ref[t]) + _dot(a1, wk1_ref[t])
        acc_v = acc_v + _dot(a1, wv1_ref[t]) + _dot(a2, wv2_ref[t])
    for g in range(N_KV):
        pk_ref[0, g] = acc_k[g * n_sub:(g + 1) * n_sub]
        pv_ref[0, g] = acc_v[g * n_sub:(g + 1) * n_sub]


def _cmp_partial(cache, page_table, w1_k, w1_v):
    n_pool, page = cache.shape[:2]
    batch, n_pages = page_table.shape
    assert page % CMP_STRIDE == 0 and n_pages % PAGES_PER_STEP == 0 and CMP_LEN == 2 * CMP_STRIDE
    assert ROW_TILES == 3 and LANE < N_DQK <= 2 * LANE < N_KVW
    sub_step = PAGES_PER_STEP * page // CMP_STRIDE
    n_sub = n_pages * page // CMP_STRIDE

    def halves(w1, d):
        w = w1.astype(BF16).reshape(2, CMP_STRIDE, d, d)
        return jnp.concatenate([w[0], w[1]], axis=-1)

    wk, wv = halves(w1_k, N_DQK), halves(w1_v, N_DV)
    zrow = lambda n, w: jnp.zeros((CMP_STRIDE, n, w.shape[-1]), BF16)
    wk0 = wk[:, :LANE]
    wk1 = jnp.concatenate([wk[:, LANE:], zrow(2 * LANE - N_DQK, wk)], axis=1)
    wv1 = jnp.concatenate([zrow(N_DQK - LANE, wv), wv[:, :2 * LANE - N_DQK]], axis=1)
    wv2 = jnp.concatenate([wv[:, 2 * LANE - N_DQK:], zrow(ROW_TILES * LANE - N_KVW, wv)], axis=1)

    const = lambda w: pl.BlockSpec(w.shape, lambda b, c, pt: (0, 0, 0))
    out = lambda d: pl.BlockSpec((1, N_KV, sub_step, 2 * d), lambda b, c, pt: (b, 0, c, 0))
    return pl.pallas_call(
        functools.partial(_cmp_partial_kernel, page=page),
        grid_spec=pltpu.PrefetchScalarGridSpec(
            num_scalar_prefetch=1,
            grid=(batch, n_pages // PAGES_PER_STEP),
            in_specs=_page_specs(page, n_pages) + [const(wk0), const(wk1), const(wv1), const(wv2)],
            out_specs=[out(N_DQK), out(N_DV)],
            scratch_shapes=[pltpu.VMEM((ROW_TILES, N_KV * PAGES_PER_STEP * page, LANE), F32)]),
        out_shape=[jax.ShapeDtypeStruct((batch, N_KV, n_sub, 2 * N_DQK), F32),
                   jax.ShapeDtypeStruct((batch, N_KV, n_sub, 2 * N_DV), F32)],
        compiler_params=pltpu.CompilerParams(
            dimension_semantics=("parallel", "arbitrary"),
            vmem_limit_bytes=VMEM_LIMIT_BYTES),
        name="cmp_partial",
    )(page_table.reshape(-1), *([_keys_minor(cache)] * PAGES_PER_STEP), wk0, wk1, wv1, wv2)


def _group_rows(refs, g):
    kt = jnp.concatenate([r[0, g, :N_DQK, :] for r in refs], axis=-1).astype(BF16)
    vt = jnp.concatenate([r[0, g, N_DQK:, :] for r in refs], axis=-1).astype(BF16)
    return kt, vt


def _group_scores(qx, krows):
    rg = qx.shape[0] // N_KV
    s = [_dot(qx[g * rg:(g + 1) * rg], krows[g][0]) for g in range(N_KV)]
    return jnp.concatenate(s, axis=0) * (N_DQK ** -0.5)


def _group_pv(p, krows):
    rg = p.shape[0] // N_KV
    return jnp.concatenate([_dot_nt(p[g * rg:(g + 1) * rg], krows[g][1]) for g in range(N_KV)], axis=0)


def _online_update(s, krows, m_sc, l_sc, acc_sc):
    m_old = m_sc[:, :1]
    m_new = jnp.maximum(m_old, jnp.max(s, axis=-1, keepdims=True))
    m_safe = jnp.where(jnp.abs(m_new) < jnp.inf, m_new, 0.0)
    alpha = jnp.exp(m_old - m_safe)
    p = jnp.exp(s - m_safe)
    l_sc[...] = jnp.broadcast_to(alpha * l_sc[:, :1] + jnp.sum(p, axis=-1, keepdims=True), l_sc.shape)
    acc_sc[...] = alpha * acc_sc[...] + _group_pv(p.astype(BF16), krows)
    m_sc[...] = jnp.broadcast_to(m_new, m_sc.shape)


def _nsa_cached_kernel(pt_ref, q_ref, sel_ref, ex_ref, news_ref, win_ref, neww_ref, *refs, t_new):
    pages = refs[:PAGES_PER_STEP]
    os_ref, ow_ref, m_sc, l_sc, acc_sc = refs[PAGES_PER_STEP:]
    c = pl.program_id(1)
    qx = q_ref[0]
    n_q = qx.shape[0]

    @pl.when(c == 0)
    def _():
        m_sc[...] = jnp.full(m_sc.shape, NEG_INF, F32)
        l_sc[...] = jnp.zeros(l_sc.shape, F32)
        acc_sc[...] = jnp.zeros(acc_sc.shape, F32)

    krows = [_group_rows(pages, g) for g in range(N_KV)]
    selx = _dot(sel_ref[0, 0], ex_ref[...])
    s = jnp.where(selx > 0.5, _group_scores(qx, krows), NEG_INF)
    _online_update(s, krows, m_sc, l_sc, acc_sc)

    @pl.when(c == pl.num_programs(1) - 1)
    def _():
        qi = lax.broadcasted_iota(jnp.int32, (n_q, 1), 0) % t_new
        nrows = [_group_rows([news_ref], g) for g in range(N_KV)]
        ki = lax.broadcasted_iota(jnp.int32, (1, news_ref.shape[3]), 1)
        s_n = jnp.where(ki <= qi, _group_scores(qx, nrows), NEG_INF)
        _online_update(s_n, nrows, m_sc, l_sc, acc_sc)
        os_ref[0] = acc_sc[...] / l_sc[:, :1]

        n_buf = win_ref.shape[3]
        wrows = [_group_rows([win_ref, neww_ref], g) for g in range(N_KV)]
        ri = lax.broadcasted_iota(jnp.int32, (1, n_buf + neww_ref.shape[3]), 1)
        vis = jnp.logical_or(jnp.logical_and(ri < n_buf, ri > qi + (n_buf - WINDOW)),
                             jnp.logical_and(ri >= n_buf, ri - n_buf <= qi))
        e_w, den_w = _exp_parts(jnp.where(vis, _group_scores(qx, wrows), NEG_INF))
        ow_ref[0] = _group_pv(e_w.astype(BF16), wrows) / den_w


def _nsa_cached(cache, page_table, qx, sel, new_slc, win_buf, new_win, *, t_new):
    page = cache.shape[1]
    batch, n_pages = page_table.shape
    n_q = qx.shape[1]
    keys_step = PAGES_PER_STEP * page
    assert keys_step % SEL_LEN == 0 and keys_step // SEL_LEN <= LANE and n_pages % PAGES_PER_STEP == 0
    assert win_buf.shape[1] >= WINDOW and t_new <= new_slc.shape[1]
    cache, new_slc, win_buf, new_win = map(_keys_minor, (cache, new_slc, win_buf, new_win))
    ex = (np.arange(keys_step)[None, :] // SEL_LEN) == np.arange(LANE)[:, None]
    per_b = lambda a: pl.BlockSpec((1,) + a.shape[1:], lambda b, c, pt: (b,) + (0,) * (a.ndim - 1))
    out = pl.BlockSpec((1, n_q, N_DV), lambda b, c, pt: (b, 0, 0))
    return pl.pallas_call(
        functools.partial(_nsa_cached_kernel, t_new=t_new),
        grid_spec=pltpu.PrefetchScalarGridSpec(
            num_scalar_prefetch=1,
            grid=(batch, n_pages // PAGES_PER_STEP),
            in_specs=[per_b(qx),
                      pl.BlockSpec((1, 1, n_q, LANE), lambda b, c, pt: (b, c, 0, 0)),
                      pl.BlockSpec((LANE, keys_step), lambda b, c, pt: (0, 0)),
                      per_b(new_slc), per_b(win_buf), per_b(new_win)]
            + _page_specs(page, n_pages),
            out_specs=[out, out],
            scratch_shapes=[pltpu.VMEM((n_q, LANE), F32), pltpu.VMEM((n_q, LANE), F32),
                            pltpu.VMEM((n_q, N_DV), F32)]),
        out_shape=[jax.ShapeDtypeStruct((batch, n_q, N_DV), F32)] * 2,
        compiler_params=pltpu.CompilerParams(
            dimension_semantics=("parallel", "arbitrary"),
            vmem_limit_bytes=VMEM_LIMIT_BYTES),
        name="nsa_cached",
    )(page_table.reshape(-1), qx, sel, jnp.asarray(ex, BF16), new_slc, win_buf, new_win,
      *([cache] * PAGES_PER_STEP))


def _router_kernel(x_ref, g_ref, whi_ref, wlo_ref, b_ref, f_ref, l_ref):
    x = x_ref[...]
    ms = jnp.mean(x * x, axis=-1, keepdims=True)
    f = x * lax.rsqrt(ms + EPS) * g_ref[...]
    f_hi, f_lo = _split_bf16(f)
    f_ref[...] = f_hi
    acc = _dot(f_hi, whi_ref[...]) + _dot(f_lo, whi_ref[...]) + _dot(f_hi, wlo_ref[...])
    l_ref[...] = acc + b_ref[...]


def _router(x, gain, w_router, b_router, *, tm):
    m, k = x.shape
    e = w_router.shape[1]
    w_hi = w_router.astype(BF16)
    w_lo = (w_router - w_hi.astype(F32)).astype(BF16)
    return pl.pallas_call(
        _router_kernel,
        grid=(m // tm,),
        in_specs=[pl.BlockSpec((tm, k), lambda i: (i, 0)),
                  pl.BlockSpec((1, k), lambda i: (0, 0)),
                  pl.BlockSpec((k, e), lambda i: (0, 0)),
                  pl.BlockSpec((k, e), lambda i: (0, 0)),
                  pl.BlockSpec((1, e), lambda i: (0, 0))],
        out_specs=[pl.BlockSpec((tm, k), lambda i: (i, 0)),
                   pl.BlockSpec((tm, e), lambda i: (i, 0))],
        out_shape=[jax.ShapeDtypeStruct((m, k), BF16),
                   jax.ShapeDtypeStruct((m, e), F32)],
        compiler_params=pltpu.CompilerParams(
            dimension_semantics=("parallel",),
            vmem_limit_bytes=VMEM_LIMIT_BYTES),
        name="moe_router",
    )(x, gain.reshape(1, k), w_hi, w_lo, b_router.reshape(1, e))


def _item_flags(blk_ref, exp_ref, st_ref, en_ref, tm):
    w = pl.program_id(1)
    e = exp_ref[w]
    b = blk_ref[w]
    prev = jnp.maximum(w - 1, 0)
    new_expert = jnp.logical_or(w == 0, exp_ref[prev] != e)
    new_block = jnp.logical_or(w == 0, blk_ref[prev] != b)
    rows = b * tm + lax.broadcasted_iota(jnp.int32, (tm, 1), 0)
    mask = jnp.logical_and(rows >= st_ref[e], rows < en_ref[e])
    return w, new_expert, new_block, mask


def _stream_expert_tiles(w, new_expert, exp_ref, nxt_ref, hbm_refs, land_refs, bf_refs, sem):
    tn = land_refs[0].shape[1]
    col = pl.multiple_of(pl.program_id(0) * tn, tn)

    def copies(e):
        return [pltpu.make_async_copy(hbm.at[e, :, pl.ds(col, tn)], land, sem.at[i])
                for i, (hbm, land) in enumerate(zip(hbm_refs, land_refs))]

    @pl.when(new_expert)
    def _():
        e = exp_ref[w]

        @pl.when(w == 0)
        def _():
            for c in copies(e):
                c.start()

        for c in copies(e):
            c.wait()
        for land, bf in zip(land_refs, bf_refs):
            bf[...] = land[...].astype(BF16)

        @pl.when(nxt_ref[w] >= 0)
        def _():
            for c in copies(nxt_ref[w]):
                c.start()


def _moe_up_kernel(blk_ref, exp_ref, st_ref, en_ref, tot_ref, nxt_ref,
                   x_ref, wg_hbm, wu_hbm, bg_ref, bu_ref, h_ref, wg_land, wu_land, wg_bf, wu_bf, sem, *, tm):
    w, new_expert, new_block, mask = _item_flags(blk_ref, exp_ref, st_ref, en_ref, tm)

    @pl.when(w < tot_ref[0])
    def _():
        _stream_expert_tiles(w, new_expert, exp_ref, nxt_ref, (wg_hbm, wu_hbm), (wg_land, wu_land),
                             (wg_bf, wu_bf), sem)
        x = x_ref[...]
        g = _dot(x, wg_bf[...]) + bg_ref[0]
        u = _dot(x, wu_bf[...]) + bu_ref[0]
        g = jnp.minimum(g, SWIGLU_LIMIT)
        u = jnp.clip(u, -SWIGLU_LIMIT, SWIGLU_LIMIT)
        hdn = (g * jax.nn.sigmoid(SWIGLU_ALPHA * g) * (u + 1.0)).astype(h_ref.dtype)
        keep = jnp.where(new_block, jnp.zeros_like(hdn), h_ref[...])
        h_ref[...] = jnp.where(mask, hdn, keep)


def _moe_down_kernel(blk_ref, exp_ref, st_ref, en_ref, tot_ref, nxt_ref,
                     h_ref, wd_hbm, bd_ref, y_ref, wd_land, wd_bf, sem, *, tm):
    w, new_expert, new_block, mask = _item_flags(blk_ref, exp_ref, st_ref, en_ref, tm)

    @pl.when(w < tot_ref[0])
    def _():
        _stream_expert_tiles(w, new_expert, exp_ref, nxt_ref, (wd_hbm,), (wd_land,), (wd_bf,), sem)
        y = _dot(h_ref[...], wd_bf[...]) + bd_ref[0]
        keep = jnp.where(new_block, jnp.zeros_like(y), y_ref[...])
        y_ref[...] = jnp.where(mask, y, keep)


def _moe_experts(x_sorted, sched, w_gate, b_gate, w_up, b_up, w_down, b_down):
    blk, exp, starts, ends, total, nxt = sched
    rows, d = x_sorted.shape
    n_e, _, d_h = w_gate.shape
    tm, tn = MOE_TM, MOE_TN
    n_items = blk.shape[0]
    params = pltpu.CompilerParams(dimension_semantics=("arbitrary", "arbitrary"),
                                  vmem_limit_bytes=VMEM_LIMIT_BYTES)

    row_spec = pl.BlockSpec((tm, d), lambda j, w, blk, exp, st, en, tot, nxt: (blk[w], 0))
    w_spec = pl.BlockSpec(memory_space=pl.ANY)
    b_spec = pl.BlockSpec((1, 1, tn), lambda j, w, blk, exp, st, en, tot, nxt: (exp[w], 0, j))
    out_spec = pl.BlockSpec((tm, tn), lambda j, w, blk, exp, st, en, tot, nxt: (blk[w], j))

    def tile_scratch(k_dim, n_w):
        return ([pltpu.VMEM((k_dim, tn), F32)] * n_w + [pltpu.VMEM((k_dim, tn), BF16)] * n_w
                + [pltpu.SemaphoreType.DMA((n_w,))])

    hidden = pl.pallas_call(
        functools.partial(_moe_up_kernel, tm=tm),
        grid_spec=pltpu.PrefetchScalarGridSpec(
            num_scalar_prefetch=6,
            grid=(d_h // tn, n_items),
            in_specs=[row_spec, w_spec, w_spec, b_spec, b_spec],
            out_specs=out_spec,
            scratch_shapes=tile_scratch(d, 2)),
        out_shape=jax.ShapeDtypeStruct((rows, d_h), BF16),
        compiler_params=params,
        name="moe_up",
    )(blk, exp, starts, ends, total, nxt, x_sorted, w_gate, w_up,
      b_gate.reshape(n_e, 1, d_h), b_up.reshape(n_e, 1, d_h))

    return pl.pallas_call(
        functools.partial(_moe_down_kernel, tm=tm),
        grid_spec=pltpu.PrefetchScalarGridSpec(
            num_scalar_prefetch=6,
            grid=(d // tn, n_items),
            in_specs=[row_spec, w_spec, b_spec],
            out_specs=out_spec,
            scratch_shapes=tile_scratch(d_h, 1)),
        out_shape=jax.ShapeDtypeStruct((rows, d), F32),
        compiler_params=params,
        name="moe_down",
    )(blk, exp, starts, ends, total, nxt, hidden, w_down, b_down.reshape(n_e, 1, d))


def _moe_schedule(top_e, tm):
    nk = top_e.size
    assert nk % tm == 0
    n_blocks = nk // tm
    n_items = n_blocks + N_EXPERTS - 1
    flat_e = top_e.reshape(nk)
    order = jnp.argsort(flat_e)
    counts = jnp.bincount(flat_e, length=N_EXPERTS).astype(jnp.int32)
    ends = jnp.cumsum(counts).astype(jnp.int32)
    starts = ends - counts
    first_blk = starts // tm
    last_blk = jnp.where(counts > 0, (ends - 1) // tm, first_blk - 1)
    n_it = last_blk - first_blk + 1
    it_end = jnp.cumsum(n_it).astype(jnp.int32)
    it_start = it_end - n_it
    total = it_end[-1]
    w = jnp.arange(n_items, dtype=jnp.int32)
    w_c = jnp.minimum(w, total - 1)
    e_w = jnp.minimum(jnp.searchsorted(it_end, w_c, side='right'), N_EXPERTS - 1).astype(jnp.int32)
    blk_w = (first_blk[e_w] + (w_c - it_start[e_w])).astype(jnp.int32)
    slot_of = jnp.zeros((nk,), jnp.int32).at[order].set(jnp.arange(nk, dtype=jnp.int32))
    ids = jnp.arange(N_EXPERTS, dtype=jnp.int32)
    later = lax.cummin(jnp.where(n_it > 0, ids, N_EXPERTS)[::-1])[::-1]
    nxt_e = jnp.concatenate([later[1:], jnp.full((1,), N_EXPERTS, jnp.int32)])
    nxt_e = jnp.where(nxt_e >= N_EXPERTS, -1, nxt_e).astype(jnp.int32)
    return order, slot_of, (blk_w, e_w, starts, ends, total.reshape(1), nxt_e[e_w])


def _moe_ffn(f_bf, logits, prm):
    n, d = f_bf.shape
    top_logit, top_e = lax.top_k(logits, TOP_K)
    gate = jax.nn.softmax(top_logit, axis=-1)
    order, slot_of, sched = _moe_schedule(top_e, MOE_TM)
    x_sorted = f_bf[order // TOP_K]
    y_sorted = _moe_experts(x_sorted, sched, prm['w_gate'], prm['b_gate'], prm['w_up'], prm['b_up'],
                            prm['w_down'], prm['b_down'])
    return jnp.einsum('nkd,nk->nd', y_sorted[slot_of].reshape(n, TOP_K, d), gate)


def _rmsnorm(x, g):
    xf = x.astype(F32)
    y = xf * lax.rsqrt(jnp.mean(xf * xf, axis=-1, keepdims=True) + EPS)
    return (y * g.astype(F32)).astype(x.dtype)


def _softcap(x, cap):
    return cap * jnp.tanh(x / cap)


def _masked_softmax(s, mask):
    s = jnp.where(mask, s.astype(F32), -jnp.inf)
    mx = jnp.max(s, axis=-1, keepdims=True)
    e = jnp.exp(s - jnp.where(jnp.isfinite(mx), mx, 0.0))
    den = jnp.sum(e, axis=-1, keepdims=True)
    return e / jnp.where(den > 0, den, 1.0)


def _mlstm_chunk(carry, inp):
    c, n, m = carry
    q, k, v, ig, lf = inp
    L = q.shape[2]
    b = jnp.cumsum(lf, axis=-1)
    causal = jnp.tril(jnp.ones((L, L), bool))
    dmat = jnp.where(causal, b[..., :, None] - b[..., None, :] + ig[..., None, :], -jnp.inf)
    inter = b + m[..., None]
    m_t = jnp.maximum(inter, jnp.max(dmat, axis=-1))
    w_intra = jnp.exp(dmat - m_t[..., None])
    w_inter = jnp.exp(inter - m_t)
    s = jnp.einsum('bhtd,bhsd->bhts', q, k) * w_intra
    num = jnp.einsum('bhts,bhsv->bhtv', s, v) + w_inter[..., None] * jnp.einsum('bhtd,bhdv->bhtv', q, c)
    den = jnp.sum(s, axis=-1) + w_inter * jnp.einsum('bhtd,bhd->bht', q, n)
    h = num / jnp.maximum(jnp.abs(den), jnp.exp(-m_t))[..., None]
    b_last = b[..., -1]
    g = b_last[..., None] - b + ig
    m_new = jnp.maximum(b_last + m, jnp.max(g, axis=-1))
    wk = jnp.exp(g - m_new[..., None])
    decay = jnp.exp(b_last + m - m_new)
    c_new = decay[..., None, None] * c + jnp.einsum('bhs,bhsd,bhsv->bhdv', wk, k, v)
    n_new = decay[..., None] * n + jnp.einsum('bhs,bhsd->bhd', wk, k)
    return (c_new, n_new, m_new), h


def _to_chunks(a, nc, L):
    a = a.astype(F32).reshape((a.shape[0], nc, L) + a.shape[2:])
    return jnp.moveaxis(jnp.moveaxis(a, 3, 2), 1, 0)


def _mlstm_mix(q, k, v, ig, lf, c0, n0, m0):
    B, T, H, _ = q.shape
    L = M_CHUNK if T % M_CHUNK == 0 else T
    nc = T // L
    xs = (_to_chunks(q, nc, L), _to_chunks(k, nc, L), _to_chunks(v, nc, L),
          _to_chunks(ig, nc, L), _to_chunks(lf, nc, L))
    init = (c0.astype(F32), n0.astype(F32), m0.astype(F32))
    (c1, n1, m1), hs = lax.scan(_mlstm_chunk, init, xs)
    h = jnp.moveaxis(jnp.moveaxis(hs, 0, 1), 2, 3).reshape(B, T, H, -1)
    return h, c1, n1, m1


def _compress_blocks(x_raw, pe, w1, w2):
    B, Tk, G, d = x_raw.shape
    r = CMP_LEN // CMP_STRIDE
    n_sub = Tk // CMP_STRIDE
    n_cmp = n_sub - r + 1
    sub = x_raw[:, :n_sub * CMP_STRIDE].reshape(B, n_sub, CMP_STRIDE, G, d)
    sub = jnp.moveaxis(sub, 3, 2).reshape(B, n_sub, G, CMP_STRIDE * d)
    parts = jnp.einsum('bngx,rxh->rbngh', sub, w1.reshape(r, CMP_STRIDE * d, -1))
    pre = parts[0, :, :n_cmp]
    for j in range(1, r):
        pre = pre + parts[j, :, j:j + n_cmp]
    pre = pre + pe.reshape(-1) @ w1
    return jax.nn.gelu(pre) @ w2


def _nsa_attend(q, q_pos0, kc, vc, ks, vs, kw, vw, kw_pos0, gates):
    B, Tq = q.shape[:2]
    QB = NSA_QBLK if Tq % NSA_QBLK == 0 else Tq
    nqb = Tq // QB
    scale = N_DQK ** -0.5
    n_cmp = kc.shape[1]
    cmp_start = jnp.arange(n_cmp) * CMP_STRIDE
    cmp_end = cmp_start + CMP_LEN - 1
    Tk = ks.shape[1]
    n_sel = -(-Tk // SEL_LEN)
    pad = n_sel * SEL_LEN - Tk
    ksb = jnp.moveaxis(jnp.pad(ks, ((0, 0), (0, pad), (0, 0), (0, 0))).reshape(B, n_sel, SEL_LEN, N_KV, N_DQK), 3, 1)
    vsb = jnp.moveaxis(jnp.pad(vs, ((0, 0), (0, pad), (0, 0), (0, 0))).reshape(B, n_sel, SEL_LEN, N_KV, N_DV), 3, 1)
    top = min(SEL_TOPK, n_sel)
    sel_start = jnp.arange(n_sel) * SEL_LEN
    overlap = ((cmp_start[:, None] <= sel_start[None, :] + SEL_LEN - 1) &
               (cmp_end[:, None] >= sel_start[None, :])).astype(F32)
    kwp = jnp.pad(kw, ((0, 0), (WINDOW, 0), (0, 0), (0, 0)))
    vwp = jnp.pad(vw, ((0, 0), (WINDOW, 0), (0, 0), (0, 0)))
    WK = WINDOW + QB
    qg = q.reshape(B, Tq, N_KV, N_HPG, N_DQK)
    gg = gates.reshape(B, Tq, N_KV, N_HPG, 3)
    gather_blocks = jax.vmap(jax.vmap(lambda blk, ix: blk[ix]))
    jj = jnp.arange(n_sel)

    def block(j):
        q0 = j * QB
        qb = lax.dynamic_slice_in_dim(qg, q0, QB, axis=1)
        gb = lax.dynamic_slice_in_dim(gg, q0, QB, axis=1)
        qpos = q_pos0 + q0 + jnp.arange(QB)
        s_c = jnp.einsum('bqghd,bngd->bghqn', qb, kc).astype(F32) * scale
        p_c = _masked_softmax(s_c, cmp_end[None, :] <= qpos[:, None])
        o_c = jnp.einsum('bghqn,bngd->bqghd', p_c.astype(vc.dtype), vc)
        imp = jnp.einsum('bghqn,nj->bgqj', p_c, overlap)
        cur = qpos // SEL_LEN
        valid = sel_start[None, :] <= qpos[:, None]
        forced = (jj[None, :] == 0) | ((jj[None, :] <= cur[:, None]) & (jj[None, :] > cur[:, None] - N_LOCAL))
        rank = jnp.where(forced, jnp.inf, jnp.where(valid, imp, -jnp.inf))
        _, idx = lax.top_k(rank, top)
        kg = gather_blocks(ksb, idx)
        vg = gather_blocks(vsb, idx).reshape(B, N_KV, QB, top * SEL_LEN, N_DV)
        kpos = (idx[..., None] * SEL_LEN + jnp.arange(SEL_LEN)).reshape(B, N_KV, QB, top * SEL_LEN)
        s_s = jnp.einsum('bqghd,bgqtkd->bghqtk', qb, kg).astype(F32) * scale
        s_s = s_s.reshape(B, N_KV, N_HPG, QB, top * SEL_LEN)
        p_s = _masked_softmax(s_s, (kpos <= qpos[None, None, :, None])[:, :, None])
        o_s = jnp.einsum('bghqx,bgqxd->bqghd', p_s.astype(vg.dtype), vg)
        start = q_pos0 + q0 - kw_pos0
        kbw = lax.dynamic_slice_in_dim(kwp, start, WK, axis=1)
        vbw = lax.dynamic_slice_in_dim(vwp, start, WK, axis=1)
        wpos = q_pos0 + q0 - WINDOW + jnp.arange(WK)
        wmask = ((wpos[None, :] <= qpos[:, None]) & (wpos[None, :] > qpos[:, None] - WINDOW) &
                 (wpos[None, :] >= kw_pos0))
        s_w = jnp.einsum('bqghd,bkgd->bghqk', qb, kbw).astype(F32) * scale
        p_w = _masked_softmax(s_w, wmask)
        o_w = jnp.einsum('bghqk,bkgd->bqghd', p_w.astype(vbw.dtype), vbw)
        out = gb[..., 0:1] * o_c + gb[..., 1:2] * o_s + gb[..., 2:3] * o_w
        return out.astype(q.dtype)

    outs = lax.map(block, jnp.arange(nqb))
    return jnp.moveaxis(outs, 0, 1).reshape(B, Tq, N_WIDTH)


def _mlstm_half(z, c0, n0, m0, prm, B, T):
    o = IN_OFFS
    mq = z[:, o[0]:o[1]].reshape(B, T, M_HEADS, M_DQK)
    mk = z[:, o[1]:o[2]].reshape(B, T, M_HEADS, M_DQK) * (M_DQK ** -0.5)
    mv = z[:, o[2]:o[3]].reshape(B, T, M_HEADS, M_DV)
    mo = z[:, o[3]:o[4]].reshape(B, T, M_WIDTH)
    mi = z[:, Z_G0:Z_G0 + M_HEADS].reshape(B, T, M_HEADS)
    mf = z[:, Z_G0 + M_HEADS:Z_G0 + 2 * M_HEADS].reshape(B, T, M_HEADS)
    ig = _softcap(mi + prm['b_mlstm_i'], GATE_CAP)
    lf = jax.nn.log_sigmoid(_softcap(mf + prm['b_mlstm_f'], GATE_CAP))
    h, c1, n1, m1 = _mlstm_mix(mq, mk, mv, ig, lf, c0, n0, m0)
    h = _rmsnorm(h, prm['mlstm_out_norm']).reshape(B, T, M_WIDTH)
    return h * jax.nn.sigmoid(mo), c1, n1, m1


def _gate_cols(z):
    return z[:, Z_G0 + 2 * M_HEADS:Z_G0 + 2 * M_HEADS + 3 * N_HEADS]


def _mix_fresh(x, prm, tm):
    B, T, _ = x.shape
    rows = B * T
    z = _matmul(x.reshape(rows, D_MODEL), prm['norm_mix'], prm['w_in_bf'], tm=tm, norm=True)
    c0 = jnp.zeros((B, M_HEADS, M_DQK, M_DV), F32)
    n0 = jnp.zeros((B, M_HEADS, M_DQK), F32)
    m0 = jnp.zeros((B, M_HEADS), F32)
    y_m, c1, n1, m1 = _mlstm_half(z, c0, n0, m0, prm, B, T)

    kvc = z[:, Z_KVC0:Z_KVC0 + KV_ROW].reshape(B, T, N_KV, N_KVW)
    kvs_rows, ks, vs = _kv_prep(z, Z_KVS0 // KV_ROW, prm['k_norm_slc'], tm=tm)
    kvw_rows, kw, vw = _kv_prep(z, Z_KVW0 // KV_ROW, prm['k_norm_win'], tm=tm)
    kc = _rmsnorm(_compress_blocks(kvc[..., :N_DQK], prm['cmp_pe_k'], prm['cmp_w1_k'], prm['cmp_w2_k']),
                  prm['k_norm_cmp'])
    vc = _compress_blocks(kvc[..., N_DQK:], prm['cmp_pe_v'], prm['cmp_w1_v'], prm['cmp_w2_v'])
    n_cmp = kc.shape[1]
    nc_pad = _round_up(n_cmp, LANE)
    kc_g = jnp.pad(jnp.moveaxis(kc, 2, 0), ((0, 0), (0, 0), (0, nc_pad - n_cmp), (0, Q_PAD - N_DQK)))
    vc_g = jnp.pad(jnp.moveaxis(vc, 2, 0), ((0, 0), (0, 0), (0, nc_pad - n_cmp), (0, 0)))
    kc_g = kc_g.astype(BF16).reshape(N_KV, B * nc_pad, Q_PAD)
    vc_g = vc_g.astype(BF16).reshape(N_KV, B * nc_pad, N_DV)
    graw = jnp.moveaxis(_gate_cols(z).reshape(rows, N_KV, 3 * N_HPG), 1, 0)
    graw = jnp.pad(graw, ((0, 0), (0, 0), (0, LANE - 3 * N_HPG)))
    y_n = _nsa_fresh(z, graw, ks, vs, kw, vw, kc_g, vc_g, prm['q_norm'], batch=B, t_len=T)

    mixed = jnp.concatenate([y_m.reshape(rows, M_WIDTH), y_n], axis=-1)
    x2 = x.reshape(rows, D_MODEL) + _matmul(mixed, None, prm['w_out_bf'], tm=tm, norm=False)
    keep = min(WINDOW, T)
    kv_win = kvw_rows.reshape(B, T, N_KV, N_KVW)[:, T - keep:]
    return x2, kvc, kvs_rows.reshape(B, T, N_KV, N_KVW), kv_win, c1, n1, m1


def _nsa_paged(q, gates, cache_cmp, cache_slc, win_buf, page_table, new_slc, new_win, prm):
    B, T = q.shape[:2]
    page = cache_cmp.shape[1]
    n_past = page_table.shape[1] * page
    assert T <= 8 and n_past % SEL_LEN == 0 and n_past % CMP_STRIDE == 0 and T < CMP_STRIDE
    assert (N_HPG * T * N_KV) % 8 == 0
    scale = N_DQK ** -0.5
    qpos = n_past + jnp.arange(T)
    qg = q.reshape(B, T, N_KV, N_HPG, N_DQK)
    gg = gates.reshape(B, T, N_KV, N_HPG, 3)

    pk, pv = _cmp_partial(cache_cmp, page_table, prm['cmp_w1_k'], prm['cmp_w1_v'])

    def finish(p, d, pe, w1, w2):
        pre = p[:, :, :-1, :d] + p[:, :, 1:, d:] + pe.reshape(-1) @ w1
        return jax.nn.gelu(pre) @ w2

    kc = _rmsnorm(finish(pk, N_DQK, prm['cmp_pe_k'], prm['cmp_w1_k'], prm['cmp_w2_k']), prm['k_norm_cmp'])
    vc = finish(pv, N_DV, prm['cmp_pe_v'], prm['cmp_w1_v'], prm['cmp_w2_v'])
    n_cmp = kc.shape[2]
    cmp_start = jnp.arange(n_cmp) * CMP_STRIDE
    cmp_end = cmp_start + CMP_LEN - 1
    s_c = jnp.einsum('bqghd,bgnd->bghqn', qg, kc).astype(F32) * scale
    p_c = _masked_softmax(s_c, cmp_end[None, :] <= qpos[:, None])
    o_c = jnp.einsum('bghqn,bgnd->bqghd', p_c, vc)

    n_sel = -(-(n_past + T) // SEL_LEN)
    top = min(SEL_TOPK, n_sel)
    sel_start = jnp.arange(n_sel) * SEL_LEN
    overlap = ((cmp_start[:, None] <= sel_start[None, :] + SEL_LEN - 1) &
               (cmp_end[:, None] >= sel_start[None, :])).astype(F32)
    imp = jnp.einsum('bghqn,nj->bgqj', p_c, overlap)
    jj = jnp.arange(n_sel)
    cur = qpos // SEL_LEN
    valid = sel_start[None, :] <= qpos[:, None]
    forced = (jj[None, :] == 0) | ((jj[None, :] <= cur[:, None]) & (jj[None, :] > cur[:, None] - N_LOCAL))
    rank = jnp.where(forced, jnp.inf, jnp.where(valid, imp, -jnp.inf))
    _, idx = lax.top_k(rank, top)
    chosen = jnp.any(idx[..., None] == jj, axis=-2)
    n_past_blk = n_past // SEL_LEN
    steps = page_table.shape[1] // PAGES_PER_STEP
    blk_step = n_past_blk // steps
    sel = jnp.broadcast_to(chosen[:, :, None, :, :n_past_blk], (B, N_KV, N_HPG, T, n_past_blk))
    sel = sel.reshape(B, N_KV * N_HPG * T, steps, blk_step)
    sel = jnp.pad(jnp.moveaxis(sel, 2, 1), ((0, 0), (0, 0), (0, 0), (0, LANE - blk_step))).astype(BF16)

    qx = jnp.transpose(qg, (0, 2, 3, 1, 4)).reshape(B, N_KV * N_HPG * T, N_DQK)
    pad8 = lambda a: jnp.pad(a, ((0, 0), (0, LANE - T), (0, 0), (0, 0)))
    o_s, o_w = _nsa_cached(cache_slc, page_table, qx.astype(BF16), sel, pad8(new_slc), win_buf, pad8(new_win),
                           t_new=T)
    to_q = lambda o: jnp.transpose(o.reshape(B, N_KV, N_HPG, T, N_DV), (0, 3, 1, 2, 4))
    out = gg[..., 0:1] * o_c + gg[..., 1:2] * to_q(o_s) + gg[..., 2:3] * to_q(o_w)
    return out.reshape(B, T, N_WIDTH)


def _mix_cached(x, cache_cmp, cache_slc, win_buf, page_table, c0, n0, m0, prm, tm):
    B, T, _ = x.shape
    rows = B * T
    z = _matmul(x.reshape(rows, D_MODEL), prm['norm_mix'], prm['w_in_bf'], tm=tm, norm=True)
    y_m, c1, n1, m1 = _mlstm_half(z, c0, n0, m0, prm, B, T)
    nq = z[:, Z_Q0:Z_KVC0].reshape(B, T, N_HEADS, Q_PAD)[..., :N_DQK]
    q = _rmsnorm(nq, prm['q_norm'])
    kvc = z[:, Z_KVC0:Z_KVC0 + KV_ROW].reshape(B, T, N_KV, N_KVW)
    kvs_rows, _, _ = _kv_prep(z, Z_KVS0 // KV_ROW, prm['k_norm_slc'], tm=tm)
    kvw_rows, _, _ = _kv_prep(z, Z_KVW0 // KV_ROW, prm['k_norm_win'], tm=tm)
    gates = jax.nn.sigmoid(_gate_cols(z)).reshape(B, T, N_HEADS, 3)
    y_n = _nsa_paged(q, gates, cache_cmp, cache_slc, win_buf, page_table,
                     kvs_rows.reshape(B, T, N_KV, N_KVW), kvw_rows.reshape(B, T, N_KV, N_KVW), prm)
    mixed = jnp.concatenate([y_m, y_n], axis=-1).reshape(rows, D_MODEL)
    x2 = x.reshape(rows, D_MODEL) + _matmul(mixed, None, prm['w_out_bf'], tm=tm, norm=False)
    n_past = page_table.shape[1] * cache_cmp.shape[1]
    keep = min(WINDOW, n_past + T)
    all_w = jnp.concatenate([win_buf, kvw_rows.reshape(B, T, N_KV, N_KVW)], axis=1)
    return (x2, kvc, kvs_rows.reshape(B, T, N_KV, N_KVW), all_w[:, all_w.shape[1] - keep:], c1, n1, m1)


def _ple_half(x3, ple, prm, tm):
    gate = _matmul(x3, prm['norm_ple'], prm['w_ple_gate_bf'], tm=tm, norm=True)
    proj = _matmul(ple, None, prm['w_ple_proj_bf'], tm=tm, norm=False)
    return x3 + jax.nn.sigmoid(gate) * proj


def kernel(x_prompt, x_sample, p_prompt, p_sample, cache_kv_cmp, cache_kv_slc, cache_kv_win, state_mlstm_C, state_mlstm_n, state_mlstm_m, page_table, norm_mix, w_in, b_mlstm_i, b_mlstm_f, mlstm_out_norm, q_norm, k_norm_cmp, k_norm_slc, k_norm_win, cmp_pe_k, cmp_w1_k, cmp_w2_k, cmp_pe_v, cmp_w1_v, cmp_w2_v, w_out, norm_ffn, w_router, b_router, w_gate, b_gate, w_up, b_up, w_down, b_down, norm_ple, w_ple_gate, w_ple_proj):
    depth = w_in.shape[0]
    assert depth == 1
    l = 0
    bp, sp, _ = x_prompt.shape
    db, ds, _ = x_sample.shape
    n_past = page_table.shape[1] * cache_kv_cmp.shape[2]
    prm = dict(norm_mix=norm_mix[l], b_mlstm_i=b_mlstm_i[l], b_mlstm_f=b_mlstm_f[l],
               mlstm_out_norm=mlstm_out_norm[l], q_norm=q_norm[l], k_norm_cmp=k_norm_cmp[l],
               k_norm_slc=k_norm_slc[l], k_norm_win=k_norm_win[l], cmp_pe_k=cmp_pe_k[l],
               cmp_w1_k=cmp_w1_k[l], cmp_w2_k=cmp_w2_k[l], cmp_pe_v=cmp_pe_v[l], cmp_w1_v=cmp_w1_v[l],
               cmp_w2_v=cmp_w2_v[l], norm_ffn=norm_ffn[l], w_router=w_router[l],
               b_router=b_router[l], w_gate=w_gate[l], b_gate=b_gate[l], w_up=w_up[l], b_up=b_up[l],
               w_down=w_down[l], b_down=b_down[l], norm_ple=norm_ple[l])
    prm['w_in_bf'] = _relayout_w_in(w_in[l])
    prm['w_out_bf'] = w_out[l].astype(BF16)
    prm['w_ple_gate_bf'] = w_ple_gate[l].astype(BF16)
    prm['w_ple_proj_bf'] = w_ple_proj[l].astype(BF16)

    tm_p, tm_s = 512, db * ds
    xp2, a1, a2, a3, a4, a5, a6 = _mix_fresh(x_prompt, prm, tm_p)
    xs2, b1, b2, b3, b4, b5, b6 = _mix_cached(x_sample, cache_kv_cmp[l], cache_kv_slc[l], cache_kv_win[l],
                                              page_table, state_mlstm_C[l], state_mlstm_n[l],
                                              state_mlstm_m[l], prm, tm_s)

    fp, lp = _router(xp2, prm['norm_ffn'], prm['w_router'], prm['b_router'], tm=tm_p)
    fs, ls = _router(xs2, prm['norm_ffn'], prm['w_router'], prm['b_router'], tm=tm_s)
    y_moe = _moe_ffn(jnp.concatenate([fp, fs], axis=0), jnp.concatenate([lp, ls], axis=0), prm)
    xp3 = xp2 + y_moe[:bp * sp]
    xs3 = xs2 + y_moe[bp * sp:]

    y_p = _ple_half(xp3, p_prompt[l].reshape(bp * sp, -1), prm, tm_p).reshape(bp, sp, D_MODEL)
    y_s = _ple_half(xs3, p_sample[l].reshape(db * ds, -1), prm, tm_s).reshape(db, ds, D_MODEL)

    st = lambda a: a[None]
    return (y_p, y_s, st(a1), st(a2), st(a3), st(a4), st(a5), st(a6),
            st(b1), st(b2), st(b3), st(b4), st(b5), st(b6))
```
